```python
import math
import jax
import jax.numpy as jnp
from jax import lax
import numpy as np

D_MODEL = 2048
BATCH = 2
SEQ = 8192
DEPTH = 2

NUM_MIXERS = 2
EPS = 1e-6

ATT_HEAD_DIM = 64
ATT_Q_HEADS = D_MODEL // ATT_HEAD_DIM
ATT_KV_HEADS = ATT_Q_HEADS // 8
ATT_GROUP = ATT_Q_HEADS // ATT_KV_HEADS
WINDOW = 128
ATT_BLOCK = WINDOW
ROPE_THETA = 10000.0
QKV_WIDTH = (ATT_Q_HEADS + 2 * ATT_KV_HEADS) * ATT_HEAD_DIM

SSM_D_INNER = 2 * D_MODEL
SSM_HEAD_DIM = 64
SSM_HEADS = SSM_D_INNER // SSM_HEAD_DIM
SSM_GROUPS = 8
SSM_HEADS_PER_GROUP = SSM_HEADS // SSM_GROUPS
SSM_STATE = 128
SSM_CONV = 4
SSM_CHUNK = 256
SSM_CONV_DIM = SSM_D_INNER + 2 * SSM_GROUPS * SSM_STATE
SSM_IN_WIDTH = SSM_D_INNER + SSM_CONV_DIM + SSM_HEADS
SSM_NORM_GROUP = SSM_D_INNER // SSM_GROUPS

D_FF = -(-(8 * D_MODEL) // (3 * 256)) * 256

N_ATTN_LAYERS = (DEPTH + NUM_MIXERS - 1) // NUM_MIXERS
N_SSM_LAYERS = DEPTH // NUM_MIXERS

kernel_name = 'hybrid_swa_sink_mamba2_swiglu'


def rms_norm(x, gain):
    xf = x.astype(jnp.float32)
    y = xf * lax.rsqrt(jnp.mean(xf * xf, axis=-1, keepdims=True) + EPS)
    return (y * gain.astype(jnp.float32)).astype(x.dtype)


def rope_tables(positions):
    inv_freq = ROPE_THETA ** (-jnp.arange(0, ATT_HEAD_DIM, 2, dtype=jnp.float32) / ATT_HEAD_DIM)
    ang = positions.astype(jnp.float32)[..., None] * inv_freq
    return jnp.cos(ang)[:, :, None, :], jnp.sin(ang)[:, :, None, :]


def apply_rope(t, cos, sin):
    tf = t.astype(jnp.float32)
    t1, t2 = jnp.split(tf, 2, axis=-1)
    return jnp.concatenate([t1 * cos - t2 * sin, t2 * cos + t1 * sin], axis=-1).astype(t.dtype)


def sliding_window_attention(h, positions, w_qkv, q_norm, k_norm, sinks, w_o):
    b, s, _ = h.shape
    qkv = h @ w_qkv
    q, k, v = jnp.split(qkv, [ATT_Q_HEADS * ATT_HEAD_DIM, (ATT_Q_HEADS + ATT_KV_HEADS) * ATT_HEAD_DIM], axis=-1)
    q = q.reshape(b, s, ATT_Q_HEADS, ATT_HEAD_DIM)
    k = k.reshape(b, s, ATT_KV_HEADS, ATT_HEAD_DIM)
    v = v.reshape(b, s, ATT_KV_HEADS, ATT_HEAD_DIM)
    q = rms_norm(q, q_norm)
    k = rms_norm(k, k_norm)
    cos, sin = rope_tables(positions)
    q = apply_rope(q, cos, sin)
    k = apply_rope(k, cos, sin)

    nb = s // ATT_BLOCK
    qb = q.reshape(b, nb, ATT_BLOCK, ATT_KV_HEADS, ATT_GROUP, ATT_HEAD_DIM)
    kb = k.reshape(b, nb, ATT_BLOCK, ATT_KV_HEADS, ATT_HEAD_DIM)
    vb = v.reshape(b, nb, ATT_BLOCK, ATT_KV_HEADS, ATT_HEAD_DIM)
    shift = ((0, 0), (1, 0), (0, 0), (0, 0), (0, 0))
    kw = jnp.concatenate([jnp.pad(kb, shift)[:, :-1], kb], axis=2)
    vw = jnp.concatenate([jnp.pad(vb, shift)[:, :-1], vb], axis=2)

    scale = ATT_HEAD_DIM ** -0.5
    scores = jnp.einsum('bnqhgd,bnkhd->bnhgqk', qb, kw).astype(jnp.float32) * scale
    qi = jnp.arange(ATT_BLOCK)[:, None]
    kj = jnp.arange(2 * ATT_BLOCK)[None, :]
    band = (kj > qi) & (kj <= qi + ATT_BLOCK)
    not_before_start = (jnp.arange(nb) > 0)[:, None, None] | (kj >= ATT_BLOCK)[None]
    valid = band[None] & not_before_start
    scores = jnp.where(valid[None, :, None, None], scores, -jnp.inf)

    sink = sinks.astype(jnp.float32).reshape(ATT_KV_HEADS, ATT_GROUP)[None, None, :, :, None, None]
    m = jnp.maximum(jnp.max(scores, axis=-1, keepdims=True), sink)
    p = jnp.exp(scores - m)
    probs = p / (jnp.sum(p, axis=-1, keepdims=True) + jnp.exp(sink - m))
    out = jnp.einsum('bnhgqk,bnkhd->bnqhgd', probs.astype(vw.dtype), vw)
    return out.reshape(b, s, ATT_Q_HEADS * ATT_HEAD_DIM) @ w_o


def causal_depthwise_conv(u, w, bias):
    out = lax.conv_general_dilated(u, w[:, None, :].astype(u.dtype), window_strides=(1,),
                                   padding=[(SSM_CONV - 1, 0)],
                                   dimension_numbers=('NWC', 'WIO', 'NWC'),
                                   feature_group_count=u.shape[-1])
    return out + bias.astype(u.dtype)


def mamba2_ssd(h, w_in, conv_w, conv_b, dt_bias, a_log, d_skip, norm_g, w_out):
    b, s, _ = h.shape
    G, R, P, N, L = SSM_GROUPS, SSM_HEADS_PER_GROUP, SSM_HEAD_DIM, SSM_STATE, SSM_CHUNK
    zxbcdt = h @ w_in
    z = zxbcdt[..., :SSM_D_INNER]
    xbc = zxbcdt[..., SSM_D_INNER:SSM_D_INNER + SSM_CONV_DIM]
    dt = zxbcdt[..., SSM_D_INNER + SSM_CONV_DIM:]
    xbc = jax.nn.silu(causal_depthwise_conv(xbc, conv_w, conv_b))
    xs, bm, cm = jnp.split(xbc, [SSM_D_INNER, SSM_D_INNER + G * N], axis=-1)

    xs = xs.astype(jnp.float32).reshape(b, s, G, R, P)
    bm = bm.astype(jnp.float32).reshape(b, s, G, N)
    cm = cm.astype(jnp.float32).reshape(b, s, G, N)
    dt = jax.nn.softplus(dt.astype(jnp.float32) + dt_bias.astype(jnp.float32)).reshape(b, s, G, R)
    a = -jnp.exp(a_log.astype(jnp.float32)).reshape(G, R)

    pad = (-s) % L
    nc = (s + pad) // L

    def to_chunks(t):
        t = jnp.pad(t, [(0, 0), (0, pad)] + [(0, 0)] * (t.ndim - 2))
        return jnp.moveaxis(t.reshape((b, nc, L) + t.shape[2:]), 1, 0)

    causal = jnp.tril(jnp.ones((L, L), dtype=bool))[None, :, :, None, None]

    def chunk_step(state, inp):
        xc, dtc, bc, cc = inp
        acum = jnp.cumsum(dtc * a, axis=1)
        seg = acum[:, :, None] - acum[:, None, :]
        decay = jnp.exp(jnp.where(causal, seg, -jnp.inf))
        xdt = xc * dtc[..., None]
        cb = jnp.einsum('blgn,bsgn->blsg', cc, bc)
        y_diag = jnp.einsum('blsgr,bsgrp->blgrp', cb[..., None] * decay, xdt)
        y_off = jnp.einsum('blgn,bgrpn->blgrp', cc, state) * jnp.exp(acum)[..., None]
        to_end = jnp.exp(acum[:, -1:] - acum)
        new_state = (state * jnp.exp(acum[:, -1])[..., None, None]
                     + jnp.einsum('bsgn,bsgr,bsgrp->bgrpn', bc, to_end, xdt))
        return new_state, y_diag + y_off

    init = jnp.zeros((b, G, R, P, N), jnp.float32)
    _, y = lax.scan(chunk_step, init, (to_chunks(xs), to_chunks(dt), to_chunks(bm), to_chunks(cm)))
    y = jnp.moveaxis(y, 0, 1).reshape(b, nc * L, G, R, P)[:, :s]
    y = y + d_skip.astype(jnp.float32).reshape(G, R)[:, :, None] * xs
    y = y.reshape(b, s, SSM_D_INNER) * jax.nn.silu(z.astype(jnp.float32))
    y = y.reshape(b, s, G, SSM_NORM_GROUP)
    y = y * lax.rsqrt(jnp.mean(y * y, axis=-1, keepdims=True) + EPS)
    y = y.reshape(b, s, SSM_D_INNER) * norm_g.astype(jnp.float32)
    return y.astype(h.dtype) @ w_out


def swiglu(h, w_gate, w_up, w_down):
    return (jax.nn.silu(h @ w_gate) * (h @ w_up)) @ w_down


def setup_inputs(seed: int = 0) -> dict:
    key = jax.random.key(seed)
    ks = jax.random.split(key, 24)
    f32 = jnp.float32
    resid = (2 * DEPTH) ** -0.5

    def nrm(k, shape, fan_in, scale=1.0):
        return jax.random.normal(k, shape, f32) * (scale * fan_in ** -0.5)

    def gain(k, shape):
        return 1.0 + 0.02 * jax.random.normal(k, shape, f32)

    x = jax.random.normal(ks[0], (BATCH, SEQ, D_MODEL), f32)
    start = jax.random.randint(ks[1], (BATCH, 1), 0, 4096)
    positions = (start + jnp.arange(SEQ)[None, :]).astype(jnp.int32)

    dt0 = jnp.exp(jax.random.uniform(ks[14], (N_SSM_LAYERS, SSM_HEADS), f32)
                  * (math.log(0.1) - math.log(0.001)) + math.log(0.001))
    return {
        'x': x,
        'positions': positions,
        'mixer_norm': gain(ks[2], (DEPTH, D_MODEL)),
        'ffn_norm': gain(ks[3], (DEPTH, D_MODEL)),
        'attn_w_qkv': nrm(ks[4], (N_ATTN_LAYERS, D_MODEL, QKV_WIDTH), D_MODEL),
        'attn_q_norm': gain(ks[5], (N_ATTN_LAYERS, ATT_HEAD_DIM)),
        'attn_k_norm': gain(ks[6], (N_ATTN_LAYERS, ATT_HEAD_DIM)),
        'attn_sinks': 0.5 * jax.random.normal(ks[7], (N_ATTN_LAYERS, ATT_Q_HEADS), f32),
        'attn_w_o': nrm(ks[8], (N_ATTN_LAYERS, ATT_Q_HEADS * ATT_HEAD_DIM, D_MODEL), ATT_Q_HEADS * ATT_HEAD_DIM, resid),
        'ssm_w_in': nrm(ks[9], (N_SSM_LAYERS, D_MODEL, SSM_IN_WIDTH), D_MODEL),
        'ssm_conv_w': nrm(ks[10], (N_SSM_LAYERS, SSM_CONV, SSM_CONV_DIM), SSM_CONV),
        'ssm_conv_b': 0.02 * jax.random.normal(ks[11], (N_SSM_LAYERS, SSM_CONV_DIM), f32),
        'ssm_dt_bias': dt0 + jnp.log(-jnp.expm1(-dt0)),
        'ssm_a_log': jnp.log(jax.random.uniform(ks[12], (N_SSM_LAYERS, SSM_HEADS), f32, 1.0, 16.0)),
        'ssm_d': gain(ks[13], (N_SSM_LAYERS, SSM_HEADS)),
        'ssm_norm': gain(ks[15], (N_SSM_LAYERS, SSM_D_INNER)),
        'ssm_w_out': nrm(ks[16], (N_SSM_LAYERS, SSM_D_INNER, D_MODEL), SSM_D_INNER, resid),
        'ffn_w_gate': nrm(ks[17], (DEPTH, D_MODEL, D_FF), D_MODEL),
        'ffn_w_up': nrm(ks[18], (DEPTH, D_MODEL, D_FF), D_MODEL),
        'ffn_w_down': nrm(ks[19], (DEPTH, D_FF, D_MODEL), D_FF, resid),
    }


def reference(x, positions, mixer_norm, ffn_norm, attn_w_qkv, attn_q_norm, attn_k_norm, attn_sinks,
              attn_w_o, ssm_w_in, ssm_conv_w, ssm_conv_b, ssm_dt_bias, ssm_a_log, ssm_d, ssm_norm,
              ssm_w_out, ffn_w_gate, ffn_w_up, ffn_w_down):
    for i in range(DEPTH):
        h = rms_norm(x, mixer_norm[i])
        j = i // NUM_MIXERS
        if i % NUM_MIXERS == 0:
            x = x + sliding_window_attention(h, positions, attn_w_qkv[j], attn_q_norm[j], attn_k_norm[j],
                                             attn_sinks[j], attn_w_o[j])
        else:
            x = x + mamba2_ssd(h, ssm_w_in[j], ssm_conv_w[j], ssm_conv_b[j], ssm_dt_bias[j], ssm_a_log[j],
                               ssm_d[j], ssm_norm[j], ssm_w_out[j])
        x = x + swiglu(rms_norm(x, ffn_norm[i]), ffn_w_gate[i], ffn_w_up[i], ffn_w_down[i])
    return x
```

```python
import functools

import jax
import jax.numpy as jnp
from jax import lax
from jax.experimental import pallas as pl
from jax.experimental.pallas import tpu as pltpu

EPS = 1e-6
HEAD_DIM = 64
ATT_GROUP = 8
WINDOW = 128
ROPE_THETA = 10000.0
SSM_GROUPS = 8
SSM_STATE = 128
SSM_CONV = 4
SSM_CHUNK = 256
LANES = 128
CONV_TAIL_ROWS = 16
VMEM_LIMIT_BYTES = 56 * 1024 * 1024

F32 = jnp.float32
BF16 = jnp.bfloat16


def _params(*sem):
    return pltpu.CompilerParams(dimension_semantics=sem, vmem_limit_bytes=VMEM_LIMIT_BYTES)


def _pick(n, candidates):
    for c in candidates:
        if n % c == 0:
            return c
    raise ValueError(f"no tile in {candidates} divides {n}")


def _rms_norm_bf16(x, gain):
    y = x * lax.rsqrt(jnp.mean(x * x, axis=-1, keepdims=True) + EPS)
    return (y * gain).astype(BF16)


def _dot(a, b):
    return jnp.dot(a, b, preferred_element_type=F32)


def _dot_nt(a, b):
    return lax.dot_general(a, b, (((1,), (1,)), ((), ())), preferred_element_type=F32)


def _dot_tn(a, b):
    return lax.dot_general(a, b, (((0,), (0,)), ((), ())), preferred_element_type=F32)


def _split3(x):
    hi = x.astype(BF16)
    r1 = x - hi.astype(F32)
    mid = r1.astype(BF16)
    lo = (r1 - mid.astype(F32)).astype(BF16)
    return hi, mid, lo


def _qkv_kernel(x_ref, pos_ref, g_ref, w_ref, invf_ref, qg_ref, kg_ref,
                q_ref, k_ref, v_ref, *, dq, dk, cw):
    h = _rms_norm_bf16(x_ref[...], g_ref[...])
    ang = pos_ref[...].astype(F32) * invf_ref[...]
    lane = lax.broadcasted_iota(jnp.int32, (1, LANES), 1)
    first_half = (lane % HEAD_DIM) < (HEAD_DIM // 2)
    cos = jnp.cos(ang)
    sin = jnp.sin(ang)
    sin_signed = jnp.where(first_half, -sin, sin)

    def head_sum(s):
        shift = HEAD_DIM // 2
        while shift >= 1:
            up = pltpu.roll(s, shift, 1)
            down = pltpu.roll(s, LANES - shift, 1)
            s = s + jnp.where((lane & shift) != 0, up, down)
            shift //= 2
        return s

    def norm_rope(a, gain, scale):
        y = a * lax.rsqrt(head_sum(a * a) * (1.0 / HEAD_DIM) + EPS) * gain
        partner = jnp.where(first_half,
                            pltpu.roll(y, LANES - HEAD_DIM // 2, 1),
                            pltpu.roll(y, HEAD_DIM // 2, 1))
        return (y * cos + partner * sin_signed) * scale

    q_scale = HEAD_DIM ** -0.5
    for c0 in range(0, dq + dk, cw):
        acc = _dot(h, w_ref[:, c0:c0 + cw])
        for s0 in range(0, cw, LANES):
            col = c0 + s0
            blk = acc[:, s0:s0 + LANES]
            if col < dq:
                q_ref[:, col:col + LANES] = norm_rope(blk, qg_ref[...], q_scale).astype(BF16)
            else:
                k_ref[:, col - dq:col - dq + LANES] = norm_rope(blk, kg_ref[...], 1.0).astype(BF16)
    for c0 in range(dq + dk, dq + 2 * dk, cw):
        v_ref[:, c0 - dq - dk:c0 - dq - dk + cw] = _dot(h, w_ref[:, c0:c0 + cw]).astype(BF16)


def _qkv_proj(x, pos, gain, w, q_gain, k_gain, *, bm):
    t, d = x.shape
    dq = d
    dk = d // ATT_GROUP
    cw = _pick(dk, (256, 128))
    inv_freq = ROPE_THETA ** (-jnp.arange(0, HEAD_DIM, 2, dtype=F32) / HEAD_DIM)
    invf = jnp.tile(inv_freq, LANES // (HEAD_DIM // 2))[None, :]
    tile_gain = lambda g: jnp.tile(g.astype(F32), LANES // HEAD_DIM)[None, :]
    row = lambda i: (i, 0)
    fixed = lambda i: (0, 0)
    return pl.pallas_call(
        functools.partial(_qkv_kernel, dq=dq, dk=dk, cw=cw),
        out_shape=(jax.ShapeDtypeStruct((t, dq), BF16),
                   jax.ShapeDtypeStruct((t, dk), BF16),
                   jax.ShapeDtypeStruct((t, dk), BF16)),
        grid=(t // bm,),
        in_specs=[pl.BlockSpec((bm, d), row),
                  pl.BlockSpec((bm, 1), row),
                  pl.BlockSpec((1, d), fixed),
                  pl.BlockSpec((d, dq + 2 * dk), fixed),
                  pl.BlockSpec((1, LANES), fixed),
                  pl.BlockSpec((1, LANES), fixed),
                  pl.BlockSpec((1, LANES), fixed)],
        out_specs=(pl.BlockSpec((bm, dq), row),
                   pl.BlockSpec((bm, dk), row),
                   pl.BlockSpec((bm, dk), row)),
        compiler_params=_params("arbitrary"),
        name="qkv_proj",
    )(x, pos, gain[None, :], w, invf, tile_gain(q_gain), tile_gain(k_gain))


def _attn_kernel(sink_ref, q_ref, kp_ref, ko_ref, vp_ref, vo_ref, o_ref, *, n_kv):
    n = pl.program_id(1)
    qi = lax.broadcasted_iota(jnp.int32, (WINDOW, WINDOW), 0)
    kj = lax.broadcasted_iota(jnp.int32, (WINDOW, WINDOW), 1)
    prev_ok = (kj > qi) & (n > 0)
    own_ok = kj <= qi
    neg = -jnp.inf
    for kv in range(n_kv):
        ks = slice(kv * HEAD_DIM, (kv + 1) * HEAD_DIM)
        kp, ko, vp, vo = kp_ref[:, ks], ko_ref[:, ks], vp_ref[:, ks], vo_ref[:, ks]
        for pair in range(ATT_GROUP // 2):
            outs = []
            for sub in range(2):
                hq = kv * ATT_GROUP + pair * 2 + sub
                q = q_ref[:, hq * HEAD_DIM:(hq + 1) * HEAD_DIM]
                sp = jnp.where(prev_ok, _dot_nt(q, kp), neg)
                so = jnp.where(own_ok, _dot_nt(q, ko), neg)
                sink = sink_ref[hq]
                m = jnp.maximum(jnp.maximum(jnp.max(sp, axis=-1, keepdims=True),
                                            jnp.max(so, axis=-1, keepdims=True)), sink)
                pp = jnp.exp(sp - m)
                po = jnp.exp(so - m)
                denom = (jnp.sum(pp, axis=-1, keepdims=True) + jnp.sum(po, axis=-1, keepdims=True)
                         + jnp.exp(sink - m))
                pv = _dot(pp.astype(BF16), vp) + _dot(po.astype(BF16), vo)
                outs.append(pv / denom)
            c0 = (kv * ATT_GROUP + pair * 2) * HEAD_DIM
            o_ref[:, c0:c0 + 2 * HEAD_DIM] = jnp.concatenate(outs, axis=1).astype(BF16)


def _attention(q, k, v, sinks, *, batch):
    t, dq = q.shape
    dk = k.shape[1]
    nb = t // batch // WINDOW
    own = lambda b, n: (b * nb + n, 0)
    prev = lambda b, n: (b * nb + jnp.maximum(n - 1, 0), 0)
    return pl.pallas_call(
        functools.partial(_attn_kernel, n_kv=dk // HEAD_DIM),
        out_shape=jax.ShapeDtypeStruct((t, dq), BF16),
        grid=(batch, nb),
        in_specs=[pl.BlockSpec(memory_space=pltpu.SMEM),
                  pl.BlockSpec((WINDOW, dq), own),
                  pl.BlockSpec((WINDOW, dk), prev),
                  pl.BlockSpec((WINDOW, dk), own),
                  pl.BlockSpec((WINDOW, dk), prev),
                  pl.BlockSpec((WINDOW, dk), own)],
        out_specs=pl.BlockSpec((WINDOW, dq), own),
        compiler_params=_params("arbitrary", "arbitrary"),
        name="swa_attention",
    )(sinks.astype(F32), q, k, k, v, v)


def _proj_residual_kernel(x_ref, a_ref, w_ref, o_ref):
    o_ref[...] = x_ref[...] + _dot(a_ref[...], w_ref[...])


def _proj_residual(x, a, w, *, bm, bn):
    t, d = x.shape
    k = a.shape[1]
    return pl.pallas_call(
        _proj_residual_kernel,
        out_shape=jax.ShapeDtypeStruct((t, d), F32),
        grid=(d // bn, t // bm),
        in_specs=[pl.BlockSpec((bm, bn), lambda j, i: (i, j)),
                  pl.BlockSpec((bm, k), lambda j, i: (i, 0)),
                  pl.BlockSpec((k, bn), lambda j, i: (0, j))],
        out_specs=pl.BlockSpec((bm, bn), lambda j, i: (i, j)),
        compiler_params=_params("arbitrary", "arbitrary"),
        name="proj_residual",
    )(x, a, w)


def _ffn_kernel(x_ref, g_ref, wg_ref, wu_ref, wd_ref, o_ref, h_ref):
    f = pl.program_id(1)

    @pl.when(f == 0)
    def _():
        x = x_ref[...]
        h_ref[...] = _rms_norm_bf16(x, g_ref[...])
        o_ref[...] = x

    h = h_ref[...]
    gate = _dot(h, wg_ref[...])
    up = _dot(h, wu_ref[...])
    act = (gate * jax.nn.sigmoid(gate) * up).astype(BF16)
    o_ref[...] += _dot(act, wd_ref[...])


def _ffn(x, gain, w_gate, w_up, w_down, *, bm, bf):
    t, d = x.shape
    dff = w_gate.shape[1]
    return pl.pallas_call(
        _ffn_kernel,
        out_shape=jax.ShapeDtypeStruct((t, d), F32),
        grid=(t // bm, dff // bf),
        in_specs=[pl.BlockSpec((bm, d), lambda i, f: (i, 0)),
                  pl.BlockSpec((1, d), lambda i, f: (0, 0)),
                  pl.BlockSpec((d, bf), lambda i, f: (0, f)),
                  pl.BlockSpec((d, bf), lambda i, f: (0, f)),
                  pl.BlockSpec((bf, d), lambda i, f: (f, 0))],
        out_specs=pl.BlockSpec((bm, d), lambda i, f: (i, 0)),
        scratch_shapes=[pltpu.VMEM((bm, d), BF16)],
        compiler_params=_params("arbitrary", "arbitrary"),
        name="swiglu_ffn",
    )(x, gain[None, :], w_gate, w_up, w_down)


def _softplus(x):
    return jnp.maximum(x, 0.0) + jnp.log1p(jnp.exp(-jnp.abs(x)))


def _ssm_in_kernel(x_ref, g_ref, w_ref, wdt_ref, dtb_ref, zx_ref, dtt_ref, h_ref):
    j = pl.program_id(1)

    @pl.when(j == 0)
    def _():
        h = _rms_norm_bf16(x_ref[...], g_ref[...])
        h_ref[...] = h
        dtt_ref[...] = _softplus(_dot_nt(wdt_ref[...], h) + dtb_ref[...])

    zx_ref[...] = _dot(h_ref[...], w_ref[...]).astype(BF16)


def _ssm_in_proj(x, gain, w_main, w_dt_t, dt_bias, *, bm, bn):
    t, d = x.shape
    n = w_main.shape[1]
    heads = w_dt_t.shape[0]
    return pl.pallas_call(
        _ssm_in_kernel,
        out_shape=(jax.ShapeDtypeStruct((t, n), BF16),
                   jax.ShapeDtypeStruct((heads, t), F32)),
        grid=(t // bm, n // bn),
        in_specs=[pl.BlockSpec((bm, d), lambda i, j: (i, 0)),
                  pl.BlockSpec((1, d), lambda i, j: (0, 0)),
                  pl.BlockSpec((d, bn), lambda i, j: (0, j)),
                  pl.BlockSpec((heads, d), lambda i, j: (0, 0)),
                  pl.BlockSpec((heads, 1), lambda i, j: (0, 0))],
        out_specs=(pl.BlockSpec((bm, bn), lambda i, j: (i, j)),
                   pl.BlockSpec((heads, bm), lambda i, j: (0, i))),
        scratch_shapes=[pltpu.VMEM((bm, d), BF16)],
        compiler_params=_params("arbitrary", "arbitrary"),
        name="ssm_in_proj",
    )(x, gain[None, :], w_main, w_dt_t, dt_bias.astype(F32)[:, None])


def _ssd_kernel(z_ref, xs_ref, b_ref, c_ref, xs_t_ref, b_t_ref, c_t_ref, dtt_ref, ex_ref,
                cwx_ref, cwb_ref, cwc_ref, cbx_ref, cbb_ref, cbc_ref,
                alog_row_ref, alog_col_ref, d_ref, ng_ref,
                y_ref, state_ref, *, chunk, rpg):
    c = pl.program_id(1)
    g = pl.program_id(2)
    L = chunk

    @pl.when(c == 0)
    def _():
        state_ref[g] = jnp.zeros(state_ref.shape[1:], F32)

    def conv_silu(u_ref, t_ref, w_ref, bias_ref):
        tail = t_ref[CONV_TAIL_ROWS - 8:, :].astype(F32)
        tail = jnp.where(c > 0, tail, 0.0)
        ext = jnp.concatenate([tail, u_ref[...].astype(F32)], axis=0)
        acc = bias_ref[...]
        for tap in range(SSM_CONV):
            off = 8 - (SSM_CONV - 1) + tap
            acc = acc + w_ref[tap:tap + 1, :] * ext[off:off + L, :]
        return acc * jax.nn.sigmoid(acc)

    xs = conv_silu(xs_ref, xs_t_ref, cwx_ref, cbx_ref)
    bm = conv_silu(b_ref, b_t_ref, cwb_ref, cbb_ref).astype(BF16)
    cm = conv_silu(c_ref, c_t_ref, cwc_ref, cbc_ref).astype(BF16)

    dtt = dtt_ref[...]
    dta_t = dtt * (-jnp.exp(alog_col_ref[...]))
    s_i = lax.broadcasted_iota(jnp.int32, (L, L), 0)
    l_i = lax.broadcasted_iota(jnp.int32, (L, L), 1)
    upper = (s_i <= l_i).astype(BF16)
    acum_t = sum(_dot(p, upper) for p in _split3(dta_t))

    ex = ex_ref[...]
    expand = lambda rows: sum(_dot_tn(p, ex) for p in _split3(rows))
    dt_full = expand(dtt)
    acum = expand(acum_t)
    acum_last = acum[L - 1:L, :]

    xdt = xs * dt_full
    xdt_b = xdt.astype(BF16)
    cb = _dot_nt(cm, bm)
    causal = l_i <= s_i
    y_parts = []
    for r in range(rpg):
        col = acum[:, r * HEAD_DIM:r * HEAD_DIM + 1]
        row = acum_t[r:r + 1, :]
        decay = jnp.exp(jnp.where(causal, col - row, -jnp.inf))
        y_parts.append(_dot((cb * decay).astype(BF16), xdt_b[:, r * HEAD_DIM:(r + 1) * HEAD_DIM]))
    y = jnp.concatenate(y_parts, axis=1)

    state = state_ref[g]
    y = y + _dot(cm, state.astype(BF16)) * jnp.exp(acum)
    to_end = jnp.exp(acum_last - acum)
    state_ref[g] = state * jnp.exp(acum_last) + _dot_tn(bm, (xdt * to_end).astype(BF16))

    y = y + d_ref[...] * xs
    z = z_ref[...].astype(F32)
    y = y * (z * jax.nn.sigmoid(z))
    y = y * lax.rsqrt(jnp.mean(y * y, axis=-1, keepdims=True) + EPS)
    y_ref[...] = (y * ng_ref[...]).astype(BF16)


def _ssd(zx, dtt, conv_w, conv_b, a_log, d_skip, norm_g, *, batch, d_inner):
    t = zx.shape[0]
    heads = dtt.shape[0]
    G, N = SSM_GROUPS, SSM_STATE
    rpg = heads // G
    gw = d_inner // G
    L = SSM_CHUNK
    nc = t // batch // L
    tail_blocks = L // CONV_TAIL_ROWS
    xs0 = d_inner // gw
    b0 = 2 * d_inner // N
    c0 = b0 + G

    def cur(off):
        return lambda b, c, g: (b * nc + c, off + g)

    def tail(off):
        return lambda b, c, g: (jnp.maximum((b * nc + c) * tail_blocks - 1, 0), off + g)

    def per_group(off=0):
        return lambda b, c, g: (0, off + g)

    ex = (jnp.arange(gw)[None, :] // HEAD_DIM == jnp.arange(rpg)[:, None]).astype(BF16)
    rep = lambda v: jnp.repeat(v.astype(F32), HEAD_DIM)[None, :]
    cw = conv_w.astype(F32)
    cb = conv_b.astype(F32)[None, :]
    return pl.pallas_call(
        functools.partial(_ssd_kernel, chunk=L, rpg=rpg),
        out_shape=jax.ShapeDtypeStruct((t, d_inner), BF16),
        grid=(batch, nc, G),
        in_specs=[pl.BlockSpec((L, gw), cur(0)),
                  pl.BlockSpec((L, gw), cur(xs0)),
                  pl.BlockSpec((L, N), cur(b0)),
                  pl.BlockSpec((L, N), cur(c0)),
                  pl.BlockSpec((CONV_TAIL_ROWS, gw), tail(xs0)),
                  pl.BlockSpec((CONV_TAIL_ROWS, N), tail(b0)),
                  pl.BlockSpec((CONV_TAIL_ROWS, N), tail(c0)),
                  pl.BlockSpec((rpg, L), lambda b, c, g: (g, b * nc + c)),
                  pl.BlockSpec((rpg, gw), lambda b, c, g: (0, 0)),
                  pl.BlockSpec((SSM_CONV, gw), per_group(0)),
                  pl.BlockSpec((SSM_CONV, N), per_group(d_inner // N)),
                  pl.BlockSpec((SSM_CONV, N), per_group(d_inner // N + G)),
                  pl.BlockSpec((1, gw), per_group(0)),
                  pl.BlockSpec((1, N), per_group(d_inner // N)),
                  pl.BlockSpec((1, N), per_group(d_inner // N + G)),
                  pl.BlockSpec((1, gw), per_group(0)),
                  pl.BlockSpec((rpg, 1), lambda b, c, g: (g, 0)),
                  pl.BlockSpec((1, gw), per_group(0)),
                  pl.BlockSpec((1, gw), per_group(0))],
        out_specs=pl.BlockSpec((L, gw), cur(0)),
        scratch_shapes=[pltpu.VMEM((G, N, gw), F32)],
        compiler_params=_params("arbitrary", "arbitrary", "arbitrary"),
        name="ssd_scan",
    )(zx, zx, zx, zx, zx, zx, zx, dtt, ex, cw, cw, cw, cb, cb, cb,
      rep(a_log), a_log.astype(F32)[:, None], rep(d_skip), norm_g.astype(F32)[None, :])


def kernel(x, positions, mixer_norm, ffn_norm, attn_w_qkv, attn_q_norm, attn_k_norm, attn_sinks,
           attn_w_o, ssm_w_in, ssm_conv_w, ssm_conv_b, ssm_dt_bias, ssm_a_log, ssm_d, ssm_norm,
           ssm_w_out, ffn_w_gate, ffn_w_up, ffn_w_down):
    batch, seq, d = x.shape
    t = batch * seq
    xf = x.reshape(t, d)
    pos = positions.reshape(t, 1)
    d_inner = ssm_w_out.shape[1]
    main_w = 2 * d_inner + 2 * SSM_GROUPS * SSM_STATE
    bm = _pick(t, (512, 256, 128))
    bf = _pick(ffn_w_gate.shape[2], (512, 256))
    bn = _pick(d, (1024, 512))

    def ffn(xf, i):
        return _ffn(xf, ffn_norm[i], ffn_w_gate[i].astype(BF16), ffn_w_up[i].astype(BF16),
                    ffn_w_down[i].astype(BF16), bm=bm, bf=bf)

    q, k, v = _qkv_proj(xf, pos, mixer_norm[0], attn_w_qkv[0].astype(BF16),
                        attn_q_norm[0], attn_k_norm[0], bm=bm)
    att = _attention(q, k, v, attn_sinks[0], batch=batch)
    xf = _proj_residual(xf, att, attn_w_o[0].astype(BF16), bm=bm, bn=bn)
    xf = ffn(xf, 0)

    w_in = ssm_w_in[0]
    zx, dtt = _ssm_in_proj(xf, mixer_norm[1], w_in[:, :main_w].astype(BF16),
                           w_in[:, main_w:].T.astype(BF16), ssm_dt_bias[0],
                           bm=bm, bn=_pick(main_w, (512, 256)))
    y = _ssd(zx, dtt, ssm_conv_w[0], ssm_conv_b[0], ssm_a_log[0], ssm_d[0], ssm_norm[0],
             batch=batch, d_inner=d_inner)
    xf = _proj_residual(xf, y, ssm_w_out[0].astype(BF16), bm=bm, bn=bn)
    xf = ffn(xf, 1)
    return xf.reshape(batch, seq, d)
```

```python
import functools

import jax
import jax.numpy as jnp
from jax import lax
from jax.experimental import pallas as pl
from jax.experimental.pallas import tpu as pltpu

EPS = 1e-6
HEAD_DIM = 64
ATT_GROUP = 8
WINDOW = 128
ROPE_THETA = 10000.0
SSM_GROUPS = 8
SSM_STATE = 128
SSM_CONV = 4
SSM_CHUNK = 256
LANES = 128
VMEM_LIMIT_BYTES = 56 * 1024 * 1024

F32 = jnp.float32
BF16 = jnp.bfloat16


def _params(*sem):
    return pltpu.CompilerParams(dimension_semantics=sem, vmem_limit_bytes=VMEM_LIMIT_BYTES)


def _pick(n, candidates):
    for c in candidates:
        if n % c == 0:
            return c
    raise ValueError(f"no tile in {candidates} divides {n}")


def _rms_norm_bf16(x, gain):
    y = x * lax.rsqrt(jnp.mean(x * x, axis=-1, keepdims=True) + EPS)
    return (y * gain).astype(BF16)


def _dot(a, b):
    return jnp.dot(a, b, preferred_element_type=F32)


def _dot_nt(a, b):
    return lax.dot_general(a, b, (((1,), (1,)), ((), ())), preferred_element_type=F32)


def _dot_tn(a, b):
    return lax.dot_general(a, b, (((0,), (0,)), ((), ())), preferred_element_type=F32)


def _qkv_kernel(x_ref, pos_ref, g_ref, w_ref, invf_ref, qg_ref, kg_ref,
                q_ref, k_ref, v_ref, acc_ref, *, dq, dk, cw):
    h = _rms_norm_bf16(x_ref[...], g_ref[...])
    ang = pos_ref[...].astype(F32) * invf_ref[...]
    lane = lax.broadcasted_iota(jnp.int32, (1, LANES), 1)
    first_half = (lane % HEAD_DIM) < (HEAD_DIM // 2)
    cos = jnp.cos(ang)
    sin = jnp.sin(ang)
    sin_signed = jnp.where(first_half, -sin, sin)

    r_i = lax.broadcasted_iota(jnp.int32, (cw, cw), 0)
    c_i = lax.broadcasted_iota(jnp.int32, (cw, cw), 1)
    same_head = (r_i // HEAD_DIM == c_i // HEAD_DIM).astype(BF16)

    def norm_rope(a, gain, scale):
        sq = a * a
        sq_hi = sq.astype(BF16)
        sq_lo = (sq - sq_hi.astype(F32)).astype(BF16)
        ss = _dot(sq_hi, same_head) + _dot(sq_lo, same_head)
        outs = []
        for s0 in range(0, cw, LANES):
            sl = slice(s0, s0 + LANES)
            y = a[:, sl] * lax.rsqrt(ss[:, sl] * (1.0 / HEAD_DIM) + EPS) * gain
            partner = jnp.where(first_half,
                                pltpu.roll(y, LANES - HEAD_DIM // 2, 1),
                                pltpu.roll(y, HEAD_DIM // 2, 1))
            outs.append(((y * cos + partner * sin_signed) * scale).astype(BF16))
        return jnp.concatenate(outs, axis=1)

    q_scale = HEAD_DIM ** -0.5
    acc_ref[...] = _dot(h, w_ref[:, :dq + dk])
    v_ref[...] = _dot(h, w_ref[:, dq + dk:]).astype(BF16)
    for c0 in range(0, dq + dk, cw):
        acc = acc_ref[:, c0:c0 + cw]
        if c0 < dq:
            q_ref[:, c0:c0 + cw] = norm_rope(acc, qg_ref[...], q_scale)
        else:
            k_ref[:, c0 - dq:c0 - dq + cw] = norm_rope(acc, kg_ref[...], 1.0)


def _qkv_proj(x, pos, gain, w, q_gain, k_gain, *, bm):
    t, d = x.shape
    dq = d
    dk = d // ATT_GROUP
    cw = _pick(dk, (256, 128))
    inv_freq = ROPE_THETA ** (-jnp.arange(0, HEAD_DIM, 2, dtype=F32) / HEAD_DIM)
    invf = jnp.tile(inv_freq, LANES // (HEAD_DIM // 2))[None, :]
    tile_gain = lambda g: jnp.tile(g.astype(F32), LANES // HEAD_DIM)[None, :]
    row = lambda i: (i, 0)
    fixed = lambda i: (0, 0)
    return pl.pallas_call(
        functools.partial(_qkv_kernel, dq=dq, dk=dk, cw=cw),
        out_shape=(jax.ShapeDtypeStruct((t, dq), BF16),
                   jax.ShapeDtypeStruct((t, dk), BF16),
                   jax.ShapeDtypeStruct((t, dk), BF16)),
        grid=(t // bm,),
        in_specs=[pl.BlockSpec((bm, d), row),
                  pl.BlockSpec((bm, 1), row),
                  pl.BlockSpec((1, d), fixed),
                  pl.BlockSpec((d, dq + 2 * dk), fixed),
                  pl.BlockSpec((1, LANES), fixed),
                  pl.BlockSpec((1, LANES), fixed),
                  pl.BlockSpec((1, LANES), fixed)],
        out_specs=(pl.BlockSpec((bm, dq), row),
                   pl.BlockSpec((bm, dk), row),
                   pl.BlockSpec((bm, dk), row)),
        scratch_shapes=[pltpu.VMEM((bm, dq + dk), F32)],
        compiler_params=_params("arbitrary"),
        name="qkv_proj",
    )(x, pos, gain[None, :], w, invf, tile_gain(q_gain), tile_gain(k_gain))


def _attn_kernel(sink_ref, q_ref, kp_ref, ko_ref, vp_ref, vo_ref, o_ref, s_ref, *, n_kv):
    n = pl.program_id(1)
    W, pairs, slab = WINDOW, ATT_GROUP // 2, 2 * HEAD_DIM
    qi = lax.broadcasted_iota(jnp.int32, (W, 2 * W), 0)
    kj = lax.broadcasted_iota(jnp.int32, (W, 2 * W), 1)
    lag = qi + W - kj
    first_key = jnp.where(n > 0, 0, W)
    valid = ((lag >= 0) & (lag < W) & (kj >= first_key))[None]
    sink_col = kj[:1] == 0
    key_row = lax.broadcasted_iota(jnp.int32, (2 * W, 1), 0)
    zeros = jnp.zeros((2 * W, HEAD_DIM), BF16)
    ones = jnp.ones((2 * W, HEAD_DIM), BF16)
    for kv in range(n_kv):
        ks = slice(kv * HEAD_DIM, (kv + 1) * HEAD_DIM)
        k = jnp.concatenate([kp_ref[:, ks], ko_ref[:, ks]], axis=0)
        v = jnp.concatenate([vp_ref[:, ks], vo_ref[:, ks]], axis=0)
        v = jnp.where(key_row == 0, jnp.zeros_like(v), v)
        c0 = kv * ATT_GROUP * HEAD_DIM
        qp = jnp.concatenate([q_ref[:, c0 + p * slab:c0 + (p + 1) * slab] for p in range(pairs)],
                             axis=0)
        s_ref[0] = _dot_nt(qp, jnp.concatenate([k, zeros], axis=1))
        s_ref[1] = _dot_nt(qp, jnp.concatenate([zeros, k], axis=1))

        def probs(parity):
            sink = jnp.concatenate(
                [jnp.full((1, 1, 1), sink_ref[kv * ATT_GROUP + 2 * p + parity], F32)
                 for p in range(pairs)], axis=0)
            fill = jnp.where(sink_col[None], sink, -jnp.inf)
            s = jnp.where(valid, s_ref[parity].reshape(pairs, W, 2 * W), fill)
            m = jnp.max(s, axis=-1, keepdims=True)
            return jnp.exp(s - m).astype(BF16).reshape(pairs * W, 2 * W)

        nd = (_dot(probs(0), jnp.concatenate([v, zeros, ones, zeros], axis=1))
              + _dot(probs(1), jnp.concatenate([zeros, v, zeros, ones], axis=1)))
        out = (nd[:, :slab] / nd[:, slab:]).astype(BF16)
        for p in range(pairs):
            o_ref[:, c0 + p * slab:c0 + (p + 1) * slab] = out[p * W:(p + 1) * W]


def _attention(q, k, v, sinks, *, batch):
    t, dq = q.shape
    dk = k.shape[1]
    nb = t // batch // WINDOW
    own = lambda b, n: (b * nb + n, 0)
    prev = lambda b, n: (b * nb + jnp.maximum(n - 1, 0), 0)
    return pl.pallas_call(
        functools.partial(_attn_kernel, n_kv=dk // HEAD_DIM),
        out_shape=jax.ShapeDtypeStruct((t, dq), BF16),
        grid=(batch, nb),
        in_specs=[pl.BlockSpec(memory_space=pltpu.SMEM),
                  pl.BlockSpec((WINDOW, dq), own),
                  pl.BlockSpec((WINDOW, dk), prev),
                  pl.BlockSpec((WINDOW, dk), own),
                  pl.BlockSpec((WINDOW, dk), prev),
                  pl.BlockSpec((WINDOW, dk), own)],
        out_specs=pl.BlockSpec((WINDOW, dq), own),
        scratch_shapes=[pltpu.VMEM((2, ATT_GROUP // 2 * WINDOW, 2 * WINDOW), F32)],
        compiler_params=_params("arbitrary", "arbitrary"),
        name="swa_attention",
    )(sinks.astype(F32), q, k, k, v, v)


def _proj_residual_kernel(x_ref, a_ref, w_ref, o_ref):
    o_ref[...] = x_ref[...] + _dot(a_ref[...], w_ref[...])


def _proj_residual(x, a, w, *, bm, bn):
    t, d = x.shape
    k = a.shape[1]
    return pl.pallas_call(
        _proj_residual_kernel,
        out_shape=jax.ShapeDtypeStruct((t, d), F32),
        grid=(d // bn, t // bm),
        in_specs=[pl.BlockSpec((bm, bn), lambda j, i: (i, j)),
                  pl.BlockSpec((bm, k), lambda j, i: (i, 0)),
                  pl.BlockSpec((k, bn), lambda j, i: (0, j))],
        out_specs=pl.BlockSpec((bm, bn), lambda j, i: (i, j)),
        compiler_params=_params("arbitrary", "arbitrary"),
        name="proj_residual",
    )(x, a, w)


def _ffn_kernel(x_ref, g_ref, wg_ref, wu_ref, wd_ref, o_ref, h_ref):
    f = pl.program_id(1)

    @pl.when(f == 0)
    def _():
        x = x_ref[...]
        h_ref[...] = _rms_norm_bf16(x, g_ref[...])
        o_ref[...] = x

    h = h_ref[...]
    gate = _dot(h, wg_ref[...])
    up = _dot(h, wu_ref[...])
    act = (gate * jax.nn.sigmoid(gate) * up).astype(BF16)
    o_ref[...] += _dot(act, wd_ref[...])


def _ffn(x, gain, w_gate, w_up, w_down, *, bm, bf):
    t, d = x.shape
    dff = w_gate.shape[1]
    return pl.pallas_call(
        _ffn_kernel,
        out_shape=jax.ShapeDtypeStruct((t, d), F32),
        grid=(t // bm, dff // bf),
        in_specs=[pl.BlockSpec((bm, d), lambda i, f: (i, 0)),
                  pl.BlockSpec((1, d), lambda i, f: (0, 0)),
                  pl.BlockSpec((d, bf), lambda i, f: (0, f)),
                  pl.BlockSpec((d, bf), lambda i, f: (0, f)),
                  pl.BlockSpec((bf, d), lambda i, f: (f, 0))],
        out_specs=pl.BlockSpec((bm, d), lambda i, f: (i, 0)),
        scratch_shapes=[pltpu.VMEM((bm, d), BF16)],
        compiler_params=_params("arbitrary", "arbitrary"),
        name="swiglu_ffn",
    )(x, gain[None, :], w_gate, w_up, w_down)


def _softplus(x):
    return jnp.maximum(x, 0.0) + jnp.log1p(jnp.exp(-jnp.abs(x)))


def _silu(x):
    return x * jax.nn.sigmoid(x)


def _ssm_gate_kernel(x_ref, g_ref, w_ref, wdt_ref, dtb_ref, zs_ref, dtt_ref, h_ref):
    @pl.when(pl.program_id(1) == 0)
    def _():
        h = _rms_norm_bf16(x_ref[...], g_ref[...])
        h_ref[...] = h
        dtt_ref[...] = _softplus(_dot_nt(wdt_ref[...], h) + dtb_ref[...])

    zs_ref[...] = _silu(_dot(h_ref[...], w_ref[...])).astype(BF16)


def _ssm_gate_proj(x, gain, w_z, w_dt_t, dt_bias, *, bm, bn):
    t, d = x.shape
    n = w_z.shape[1]
    heads = w_dt_t.shape[0]
    return pl.pallas_call(
        _ssm_gate_kernel,
        out_shape=(jax.ShapeDtypeStruct((t, n), BF16),
                   jax.ShapeDtypeStruct((heads, t), F32)),
        grid=(t // bm, n // bn),
        in_specs=[pl.BlockSpec((bm, d), lambda i, j: (i, 0)),
                  pl.BlockSpec((1, d), lambda i, j: (0, 0)),
                  pl.BlockSpec((d, bn), lambda i, j: (0, j)),
                  pl.BlockSpec((heads, d), lambda i, j: (0, 0)),
                  pl.BlockSpec((heads, 1), lambda i, j: (0, 0))],
        out_specs=(pl.BlockSpec((bm, bn), lambda i, j: (i, j)),
                   pl.BlockSpec((heads, bm), lambda i, j: (0, i))),
        scratch_shapes=[pltpu.VMEM((bm, d), BF16)],
        compiler_params=_params("arbitrary", "arbitrary"),
        name="ssm_gate_proj",
    )(x, gain[None, :], w_z, w_dt_t, dt_bias.astype(F32)[:, None])


def _ssm_conv_kernel(x_ref, g_ref, w_ref, cw_ref, cb_ref, o_ref, h_ref, acc_ref, carry_ref,
                     *, tiles_per_seq):
    i = pl.program_id(0)
    j = pl.program_id(1)
    bm = x_ref.shape[0]

    @pl.when(j == 0)
    def _():
        h_ref[...] = _rms_norm_bf16(x_ref[...], g_ref[...])

    acc_ref[...] = _dot(h_ref[...], w_ref[...])
    acc = acc_ref[...]
    row8 = lax.broadcasted_iota(jnp.int32, (8, 1), 0)
    tail = jnp.where(i % tiles_per_seq != 0, carry_ref[j], 0.0)
    carry_ref[j] = acc[bm - 8:, :]
    out = cb_ref[...] + cw_ref[SSM_CONV - 1:SSM_CONV, :] * acc
    for back in range(1, SSM_CONV):
        rolled = pltpu.roll(acc, back, 0)
        head = jnp.where(row8 < back, pltpu.roll(tail, back, 0), rolled[:8])
        shifted = jnp.concatenate([head, rolled[8:]], axis=0)
        out = out + cw_ref[SSM_CONV - 1 - back:SSM_CONV - back, :] * shifted
    o_ref[...] = _silu(out).astype(BF16)


def _ssm_conv_proj(x, gain, w_xbc, conv_w, conv_b, *, bm, bn, seq):
    t, d = x.shape
    n = w_xbc.shape[1]
    return pl.pallas_call(
        functools.partial(_ssm_conv_kernel, tiles_per_seq=seq // bm),
        out_shape=jax.ShapeDtypeStruct((t, n), BF16),
        grid=(t // bm, n // bn),
        in_specs=[pl.BlockSpec((bm, d), lambda i, j: (i, 0)),
                  pl.BlockSpec((1, d), lambda i, j: (0, 0)),
                  pl.BlockSpec((d, bn), lambda i, j: (0, j)),
                  pl.BlockSpec((SSM_CONV, bn), lambda i, j: (0, j)),
                  pl.BlockSpec((1, bn), lambda i, j: (0, j))],
        out_specs=pl.BlockSpec((bm, bn), lambda i, j: (i, j)),
        scratch_shapes=[pltpu.VMEM((bm, d), BF16), pltpu.VMEM((bm, bn), F32),
                        pltpu.VMEM((n // bn, 8, bn), F32)],
        compiler_params=_params("arbitrary", "arbitrary"),
        name="ssm_conv_proj",
    )(x, gain[None, :], w_xbc, conv_w.astype(F32), conv_b.astype(F32)[None, :])


def _bf16_terms(x, n):
    terms = []
    for _ in range(n):
        t = x.astype(BF16).astype(F32)
        terms.append(t)
        x = x - t
    return terms


def _ssd_kernel(zs_ref, xs_ref, b_ref, c_ref, dtt_ref, ex_ref, alog_ref, d_ref, ng_ref,
                y_ref, state_ref, *, chunk, rpg):
    c = pl.program_id(1)
    g = pl.program_id(2)
    L, H = chunk, chunk // 2
    gw = xs_ref.shape[1]

    @pl.when(c == 0)
    def _():
        state_ref[g] = jnp.zeros(state_ref.shape[1:], F32)

    xs = xs_ref[...].astype(F32)
    bm = b_ref[...]
    cm = c_ref[...]

    dtt = dtt_ref[...]
    dta_t = dtt * (-jnp.exp(alog_ref[...]))
    r_i = lax.broadcasted_iota(jnp.int32, (L, L), 0)
    c_i = lax.broadcasted_iota(jnp.int32, (L, L), 1)
    upper = (r_i <= c_i).astype(BF16)
    sums = _dot(jnp.concatenate(_bf16_terms(dta_t, 3), axis=0).astype(BF16), upper)
    acum_t = sums[:rpg] + sums[rpg:2 * rpg] + sums[2 * rpg:]

    rows = jnp.concatenate(_bf16_terms(acum_t, 3) + _bf16_terms(dtt, 2), axis=0).astype(BF16)
    both = _dot_tn(rows, ex_ref[...])
    acum = both[:, :gw]
    dt_full = both[:, gw:]
    acum_last = acum[L - 1:L, :]

    xdt = xs * dt_full
    xdt_b = xdt.astype(BF16)
    cb = _dot_nt(cm, bm)
    tri = (lax.broadcasted_iota(jnp.int32, (H, H), 0) >= lax.broadcasted_iota(jnp.int32, (H, H), 1))
    y_parts = []
    for r in range(rpg):
        col = acum[:, r * HEAD_DIM:r * HEAD_DIM + 1]
        row = acum_t[r:r + 1, :]
        d00 = jnp.exp(jnp.where(tri, col[:H] - row[:, :H], -jnp.inf))
        d11 = jnp.exp(jnp.where(tri, col[H:] - row[:, H:], -jnp.inf))
        d10 = jnp.exp(col[H:] - row[:, :H])
        m00 = (cb[:H, :H] * d00).astype(BF16)
        m1 = jnp.concatenate([cb[H:, :H] * d10, cb[H:, H:] * d11], axis=1).astype(BF16)
        x_r = xdt_b[:, r * HEAD_DIM:(r + 1) * HEAD_DIM]
        y_parts.append(jnp.concatenate([_dot(m00, x_r[:H]), _dot(m1, x_r)], axis=0))
    y = jnp.concatenate(y_parts, axis=1)

    state = state_ref[g]
    y = y + _dot(cm, state.astype(BF16)) * jnp.exp(acum)
    to_end = jnp.exp(acum_last - acum)
    state_ref[g] = state * jnp.exp(acum_last) + _dot_tn(bm, (xdt * to_end).astype(BF16))

    y = (y + d_ref[...] * xs) * zs_ref[...].astype(F32)
    y = y * lax.rsqrt(jnp.mean(y * y, axis=-1, keepdims=True) + EPS)
    y_ref[...] = (y * ng_ref[...]).astype(BF16)


def _ssd(zs, xbc, dtt, a_log, d_skip, norm_g, *, batch):
    t, d_inner = zs.shape
    heads = dtt.shape[0]
    G, N = SSM_GROUPS, SSM_STATE
    rpg = heads // G
    gw = d_inner // G
    L = SSM_CHUNK
    nc = t // batch // L
    b0 = d_inner // N
    c0 = b0 + G

    def cur(off):
        return lambda b, c, g: (b * nc + c, off + g)

    per_group = lambda b, c, g: (0, g)
    one_hot = (jnp.arange(gw)[None, :] // HEAD_DIM == jnp.arange(rpg)[:, None]).astype(BF16)
    zero = jnp.zeros_like(one_hot)
    ex = jnp.concatenate([jnp.concatenate([one_hot, zero], axis=1)] * 3
                         + [jnp.concatenate([zero, one_hot], axis=1)] * 2, axis=0)
    rep = lambda v: jnp.repeat(v.astype(F32), HEAD_DIM)[None, :]
    return pl.pallas_call(
        functools.partial(_ssd_kernel, chunk=L, rpg=rpg),
        out_shape=jax.ShapeDtypeStruct((t, d_inner), BF16),
        grid=(batch, nc, G),
        in_specs=[pl.BlockSpec((L, gw), cur(0)),
                  pl.BlockSpec((L, gw), cur(0)),
                  pl.BlockSpec((L, N), cur(b0)),
                  pl.BlockSpec((L, N), cur(c0)),
                  pl.BlockSpec((rpg, L), lambda b, c, g: (g, b * nc + c)),
                  pl.BlockSpec((5 * rpg, 2 * gw), lambda b, c, g: (0, 0)),
                  pl.BlockSpec((rpg, 1), lambda b, c, g: (g, 0)),
                  pl.BlockSpec((1, gw), per_group),
                  pl.BlockSpec((1, gw), per_group)],
        out_specs=pl.BlockSpec((L, gw), cur(0)),
        scratch_shapes=[pltpu.VMEM((G, N, gw), F32)],
        compiler_params=_params("arbitrary", "arbitrary", "arbitrary"),
        name="ssd_scan",
    )(zs, xbc, xbc, xbc, dtt, ex, a_log.astype(F32)[:, None], rep(d_skip), norm_g.astype(F32)[None, :])


def kernel(x, positions, mixer_norm, ffn_norm, attn_w_qkv, attn_q_norm, attn_k_norm, attn_sinks,
           attn_w_o, ssm_w_in, ssm_conv_w, ssm_conv_b, ssm_dt_bias, ssm_a_log, ssm_d, ssm_norm,
           ssm_w_out, ffn_w_gate, ffn_w_up, ffn_w_down):
    batch, seq, d = x.shape
    t = batch * seq
    xf = x.reshape(t, d)
    pos = positions.reshape(t, 1)
    d_inner = ssm_w_out.shape[1]
    main_w = 2 * d_inner + 2 * SSM_GROUPS * SSM_STATE
    bm = _pick(t, (512, 256, 128))
    bm_big = _pick(seq, (1024, 512, 256, 128))
    bf = _pick(ffn_w_gate.shape[2], (512, 256))
    bn = _pick(d, (1024, 512))

    def ffn(xf, i):
        return _ffn(xf, ffn_norm[i], ffn_w_gate[i].astype(BF16), ffn_w_up[i].astype(BF16),
                    ffn_w_down[i].astype(BF16), bm=bm_big, bf=bf)

    q, k, v = _qkv_proj(xf, pos, mixer_norm[0], attn_w_qkv[0].astype(BF16),
                        attn_q_norm[0], attn_k_norm[0], bm=bm)
    att = _attention(q, k, v, attn_sinks[0], batch=batch)
    xf = _proj_residual(xf, att, attn_w_o[0].astype(BF16), bm=bm, bn=bn)
    xf = ffn(xf, 0)

    w_in = ssm_w_in[0]
    bn_in = _pick(SSM_GROUPS * SSM_STATE, (1024, 512, 256))
    zs, dtt = _ssm_gate_proj(xf, mixer_norm[1], w_in[:, :d_inner].astype(BF16),
                             w_in[:, main_w:].T.astype(BF16), ssm_dt_bias[0], bm=bm_big, bn=bn_in)
    xbc = _ssm_conv_proj(xf, mixer_norm[1], w_in[:, d_inner:main_w].astype(BF16),
                         ssm_conv_w[0], ssm_conv_b[0], bm=bm_big, bn=bn_in, seq=seq)
    y = _ssd(zs, xbc, dtt, ssm_a_log[0], ssm_d[0], ssm_norm[0], batch=batch)
    xf = _proj_residual(xf, y, ssm_w_out[0].astype(BF16), bm=bm, bn=bn)
    xf = ffn(xf, 1)
    return xf.reshape(batch, seq, d)
```

```python
import functools

import jax
import jax.numpy as jnp
from jax import lax
from jax.experimental import pallas as pl
from jax.experimental.pallas import tpu as pltpu

EPS = 1e-6
HEAD_DIM = 64
ATT_GROUP = 8
WINDOW = 128
ROPE_THETA = 10000.0
SSM_GROUPS = 8
SSM_STATE = 128
SSM_CONV = 4
SSM_CHUNK = 256
LANES = 128
VMEM_LIMIT_BYTES = 56 * 1024 * 1024

F32 = jnp.float32
BF16 = jnp.bfloat16


def _params(*sem):
    return pltpu.CompilerParams(dimension_semantics=sem, vmem_limit_bytes=VMEM_LIMIT_BYTES)


def _pick(n, candidates):
    for c in candidates:
        if n % c == 0:
            return c
    raise ValueError(f"no tile in {candidates} divides {n}")


def _rms_norm_bf16(x, gain):
    y = x * lax.rsqrt(jnp.mean(x * x, axis=-1, keepdims=True) + EPS)
    return (y * gain).astype(BF16)


def _dot(a, b):
    return jnp.dot(a, b, preferred_element_type=F32)


def _dot_nt(a, b):
    return lax.dot_general(a, b, (((1,), (1,)), ((), ())), preferred_element_type=F32)


def _dot_tn(a, b):
    return lax.dot_general(a, b, (((0,), (0,)), ((), ())), preferred_element_type=F32)


def _qkv_kernel(x_ref, pos_ref, g_ref, w_ref, invf_ref, qg_ref, kg_ref,
                q_ref, k_ref, v_ref, acc_ref, *, dq, dk, cw):
    h = _rms_norm_bf16(x_ref[...], g_ref[...])
    ang = pos_ref[...].astype(F32) * invf_ref[...]
    lane = lax.broadcasted_iota(jnp.int32, (1, LANES), 1)
    first_half = (lane % HEAD_DIM) < (HEAD_DIM // 2)
    cos = jnp.cos(ang)
    sin = jnp.sin(ang)
    sin_signed = jnp.where(first_half, -sin, sin)

    r_i = lax.broadcasted_iota(jnp.int32, (cw, cw), 0)
    c_i = lax.broadcasted_iota(jnp.int32, (cw, cw), 1)
    same_head = (r_i // HEAD_DIM == c_i // HEAD_DIM).astype(BF16)

    def norm_rope(a, gain, scale):
        sq = a * a
        sq_hi = sq.astype(BF16)
        sq_lo = (sq - sq_hi.astype(F32)).astype(BF16)
        ss = _dot(sq_hi, same_head) + _dot(sq_lo, same_head)
        outs = []
        for s0 in range(0, cw, LANES):
            sl = slice(s0, s0 + LANES)
            y = a[:, sl] * lax.rsqrt(ss[:, sl] * (1.0 / HEAD_DIM) + EPS) * gain
            partner = jnp.where(first_half,
                                pltpu.roll(y, LANES - HEAD_DIM // 2, 1),
                                pltpu.roll(y, HEAD_DIM // 2, 1))
            outs.append(((y * cos + partner * sin_signed) * scale).astype(BF16))
        return jnp.concatenate(outs, axis=1)

    q_scale = HEAD_DIM ** -0.5
    acc_ref[...] = _dot(h, w_ref[:, :dq + dk])
    v_ref[...] = _dot(h, w_ref[:, dq + dk:]).astype(BF16)
    for c0 in range(0, dq + dk, cw):
        acc = acc_ref[:, c0:c0 + cw]
        if c0 < dq:
            q_ref[:, c0:c0 + cw] = norm_rope(acc, qg_ref[...], q_scale)
        else:
            k_ref[:, c0 - dq:c0 - dq + cw] = norm_rope(acc, kg_ref[...], 1.0)


def _qkv_proj(x, pos, gain, w, q_gain, k_gain, *, bm):
    t, d = x.shape
    dq = d
    dk = d // ATT_GROUP
    cw = _pick(dk, (256, 128))
    inv_freq = ROPE_THETA ** (-jnp.arange(0, HEAD_DIM, 2, dtype=F32) / HEAD_DIM)
    invf = jnp.tile(inv_freq, LANES // (HEAD_DIM // 2))[None, :]
    tile_gain = lambda g: jnp.tile(g.astype(F32), LANES // HEAD_DIM)[None, :]
    row = lambda i: (i, 0)
    fixed = lambda i: (0, 0)
    return pl.pallas_call(
        functools.partial(_qkv_kernel, dq=dq, dk=dk, cw=cw),
        out_shape=(jax.ShapeDtypeStruct((t, dq), BF16),
                   jax.ShapeDtypeStruct((t, dk), BF16),
                   jax.ShapeDtypeStruct((t, dk), BF16)),
        grid=(t // bm,),
        in_specs=[pl.BlockSpec((bm, d), row),
                  pl.BlockSpec((bm, 1), row),
                  pl.BlockSpec((1, d), fixed),
                  pl.BlockSpec((d, dq + 2 * dk), fixed),
                  pl.BlockSpec((1, LANES), fixed),
                  pl.BlockSpec((1, LANES), fixed),
                  pl.BlockSpec((1, LANES), fixed)],
        out_specs=(pl.BlockSpec((bm, dq), row),
                   pl.BlockSpec((bm, dk), row),
                   pl.BlockSpec((bm, dk), row)),
        scratch_shapes=[pltpu.VMEM((bm, dq + dk), F32)],
        compiler_params=_params("arbitrary"),
        name="qkv_proj",
    )(x, pos, gain[None, :], w, invf, tile_gain(q_gain), tile_gain(k_gain))


def _attn_kernel(sink_ref, q_ref, kp_ref, ko_ref, vp_ref, vo_ref, o_ref, s_ref, *, n_kv):
    n = pl.program_id(1)
    W, pairs, slab = WINDOW, ATT_GROUP // 2, 2 * HEAD_DIM
    qi = lax.broadcasted_iota(jnp.int32, (W, 2 * W), 0)
    kj = lax.broadcasted_iota(jnp.int32, (W, 2 * W), 1)
    lag = qi + W - kj
    first_key = jnp.where(n > 0, 0, W)
    valid = ((lag >= 0) & (lag < W) & (kj >= first_key))[None]
    sink_col = kj[:1] == 0
    key_row = lax.broadcasted_iota(jnp.int32, (2 * W, 1), 0)
    zeros = jnp.zeros((2 * W, HEAD_DIM), BF16)
    ones = jnp.ones((2 * W, HEAD_DIM), BF16)
    for kv in range(n_kv):
        ks = slice(kv * HEAD_DIM, (kv + 1) * HEAD_DIM)
        k = jnp.concatenate([kp_ref[:, ks], ko_ref[:, ks]], axis=0)
        v = jnp.concatenate([vp_ref[:, ks], vo_ref[:, ks]], axis=0)
        v = jnp.where(key_row == 0, jnp.zeros_like(v), v)
        c0 = kv * ATT_GROUP * HEAD_DIM
        qp = jnp.concatenate([q_ref[:, c0 + p * slab:c0 + (p + 1) * slab] for p in range(pairs)],
                             axis=0)
        s_ref[0] = _dot_nt(qp, jnp.concatenate([k, zeros], axis=1))
        s_ref[1] = _dot_nt(qp, jnp.concatenate([zeros, k], axis=1))

        def probs(parity):
            sink = jnp.concatenate(
                [jnp.full((1, 1, 1), sink_ref[kv * ATT_GROUP + 2 * p + parity], F32)
                 for p in range(pairs)], axis=0)
            fill = jnp.where(sink_col[None], sink, -jnp.inf)
            s = jnp.where(valid, s_ref[parity].reshape(pairs, W, 2 * W), fill)
            m = jnp.max(s, axis=-1, keepdims=True)
            return jnp.exp(s - m).astype(BF16).reshape(pairs * W, 2 * W)

        nd = (_dot(probs(0), jnp.concatenate([v, zeros, ones, zeros], axis=1))
              + _dot(probs(1), jnp.concatenate([zeros, v, zeros, ones], axis=1)))
        out = (nd[:, :slab] / nd[:, slab:]).astype(BF16)
        for p in range(pairs):
            o_ref[:, c0 + p * slab:c0 + (p + 1) * slab] = out[p * W:(p + 1) * W]


def _attention(q, k, v, sinks, *, batch):
    t, dq = q.shape
    dk = k.shape[1]
    nb = t // batch // WINDOW
    own = lambda b, n: (b * nb + n, 0)
    prev = lambda b, n: (b * nb + jnp.maximum(n - 1, 0), 0)
    return pl.pallas_call(
        functools.partial(_attn_kernel, n_kv=dk // HEAD_DIM),
        out_shape=jax.ShapeDtypeStruct((t, dq), BF16),
        grid=(batch, nb),
        in_specs=[pl.BlockSpec(memory_space=pltpu.SMEM),
                  pl.BlockSpec((WINDOW, dq), own),
                  pl.BlockSpec((WINDOW, dk), prev),
                  pl.BlockSpec((WINDOW, dk), own),
                  pl.BlockSpec((WINDOW, dk), prev),
                  pl.BlockSpec((WINDOW, dk), own)],
        out_specs=pl.BlockSpec((WINDOW, dq), own),
        scratch_shapes=[pltpu.VMEM((2, ATT_GROUP // 2 * WINDOW, 2 * WINDOW), F32)],
        compiler_params=_params("arbitrary", "arbitrary"),
        name="swa_attention",
    )(sinks.astype(F32), q, k, k, v, v)


def _proj_residual_kernel(x_ref, a_ref, w_ref, o_ref):
    o_ref[...] = x_ref[...] + _dot(a_ref[...], w_ref[...])


def _proj_residual(x, a, w, *, bm, bn):
    t, d = x.shape
    k = a.shape[1]
    return pl.pallas_call(
        _proj_residual_kernel,
        out_shape=jax.ShapeDtypeStruct((t, d), F32),
        grid=(d // bn, t // bm),
        in_specs=[pl.BlockSpec((bm, bn), lambda j, i: (i, j)),
                  pl.BlockSpec((bm, k), lambda j, i: (i, 0)),
                  pl.BlockSpec((k, bn), lambda j, i: (0, j))],
        out_specs=pl.BlockSpec((bm, bn), lambda j, i: (i, j)),
        compiler_params=_params("arbitrary", "arbitrary"),
        name="proj_residual",
    )(x, a, w)


def _ffn_kernel(x_ref, g_ref, wg_ref, wu_ref, wd_ref, o_ref, h_ref):
    f = pl.program_id(1)

    @pl.when(f == 0)
    def _():
        x = x_ref[...]
        h_ref[...] = _rms_norm_bf16(x, g_ref[...])
        o_ref[...] = x

    h = h_ref[...]
    gate = _dot(h, wg_ref[...])
    up = _dot(h, wu_ref[...])
    act = (gate * jax.nn.sigmoid(gate) * up).astype(BF16)
    o_ref[...] += _dot(act, wd_ref[...])


def _ffn(x, gain, w_gate, w_up, w_down, *, layer, bm, bf):
    t, d = x.shape
    dff = w_gate.shape[2]
    return pl.pallas_call(
        _ffn_kernel,
        out_shape=jax.ShapeDtypeStruct((t, d), F32),
        grid=(t // bm, dff // bf),
        in_specs=[pl.BlockSpec((bm, d), lambda i, f: (i, 0)),
                  pl.BlockSpec((1, d), lambda i, f: (0, 0)),
                  pl.BlockSpec((None, d, bf), lambda i, f: (layer, 0, f)),
                  pl.BlockSpec((None, d, bf), lambda i, f: (layer, 0, f)),
                  pl.BlockSpec((None, bf, d), lambda i, f: (layer, f, 0))],
        out_specs=pl.BlockSpec((bm, d), lambda i, f: (i, 0)),
        scratch_shapes=[pltpu.VMEM((bm, d), BF16)],
        compiler_params=_params("arbitrary", "arbitrary"),
        name="swiglu_ffn",
    )(x, gain[None, :], w_gate, w_up, w_down)


def _softplus(x):
    return jnp.maximum(x, 0.0) + jnp.log1p(jnp.exp(-jnp.abs(x)))


def _silu(x):
    return x * jax.nn.sigmoid(x)


def _bf16_terms(x, n):
    terms = []
    for _ in range(n):
        t = x.astype(BF16).astype(F32)
        terms.append(t)
        x = x - t
    return terms


def _ssm_gate_kernel(x_ref, g_ref, w_ref, wdt_ref, wdtt_ref, dtb_row_ref, dtb_col_ref,
                     alog_row_ref, alog_col_ref, zs_ref, acum_t_ref, acum3_ref, dt2_ref, h_ref,
                     *, chunk):
    @pl.when(pl.program_id(1) == 0)
    def _():
        h = _rms_norm_bf16(x_ref[...], g_ref[...])
        h_ref[...] = h
        heads = wdtt_ref.shape[0]
        L = chunk
        dt_c = _softplus(_dot(h, wdt_ref[...]) + dtb_row_ref[...])
        dt_t = _softplus(_dot_nt(wdtt_ref[...], h) + dtb_col_ref[...])
        dta_c = dt_c * (-jnp.exp(alog_row_ref[...]))
        dta_t = dt_t * (-jnp.exp(alog_col_ref[...]))
        r_i = lax.broadcasted_iota(jnp.int32, (L, L), 0)
        c_i = lax.broadcasted_iota(jnp.int32, (L, L), 1)
        upper = (r_i <= c_i).astype(BF16)
        lower = (r_i >= c_i).astype(BF16)
        for r0 in range(0, x_ref.shape[0], L):
            rows = slice(r0, r0 + L)
            stacked = jnp.concatenate(_bf16_terms(dta_t[:, rows], 3), axis=0).astype(BF16)
            sums = _dot(stacked, upper)
            acum_t_ref[:, rows] = sums[:heads] + sums[heads:2 * heads] + sums[2 * heads:]
            acum_c = sum(_dot(lower, term.astype(BF16)) for term in _bf16_terms(dta_c[rows], 3))
            acum3_ref[rows, :] = jnp.concatenate(_bf16_terms(acum_c, 3), axis=1).astype(BF16)
        dt2_ref[...] = jnp.concatenate(_bf16_terms(dt_c, 2), axis=1).astype(BF16)

    zs_ref[...] = _silu(_dot(h_ref[...], w_ref[...])).astype(BF16)


def _ssm_gate_proj(x, gain, w_in, w_dt, dt_bias, a_log, *, bm, bn, n):
    t, d = x.shape
    heads = w_dt.shape[1]
    fixed = lambda i, j: (0, 0)
    row_tile = lambda i, j: (i, 0)
    f32 = lambda v: v.astype(F32)
    return pl.pallas_call(
        functools.partial(_ssm_gate_kernel, chunk=SSM_CHUNK),
        out_shape=(jax.ShapeDtypeStruct((t, n), BF16),
                   jax.ShapeDtypeStruct((heads, t), F32),
                   jax.ShapeDtypeStruct((t, 3 * heads), BF16),
                   jax.ShapeDtypeStruct((t, 2 * heads), BF16)),
        grid=(t // bm, n // bn),
        in_specs=[pl.BlockSpec((bm, d), row_tile),
                  pl.BlockSpec((1, d), fixed),
                  pl.BlockSpec((d, bn), lambda i, j: (0, j)),
                  pl.BlockSpec((d, heads), fixed),
                  pl.BlockSpec((heads, d), fixed),
                  pl.BlockSpec((1, heads), fixed),
                  pl.BlockSpec((heads, 1), fixed),
                  pl.BlockSpec((1, heads), fixed),
                  pl.BlockSpec((heads, 1), fixed)],
        out_specs=(pl.BlockSpec((bm, bn), lambda i, j: (i, j)),
                   pl.BlockSpec((heads, bm), lambda i, j: (0, i)),
                   pl.BlockSpec((bm, 3 * heads), row_tile),
                   pl.BlockSpec((bm, 2 * heads), row_tile)),
        scratch_shapes=[pltpu.VMEM((bm, d), BF16)],
        compiler_params=_params("arbitrary", "arbitrary"),
        name="ssm_gate_proj",
    )(x, gain[None, :], w_in, w_dt, w_dt.T, f32(dt_bias)[None, :], f32(dt_bias)[:, None],
      f32(a_log)[None, :], f32(a_log)[:, None])


def _ssm_conv_kernel(x_ref, g_ref, w_ref, cw_ref, cb_ref, o_ref, h_ref, acc_ref, carry_ref,
                     *, tiles_per_seq):
    i = pl.program_id(0)
    j = pl.program_id(1)
    bm = x_ref.shape[0]

    @pl.when(j == 0)
    def _():
        h_ref[...] = _rms_norm_bf16(x_ref[...], g_ref[...])

    acc_ref[...] = _dot(h_ref[...], w_ref[...])
    acc = acc_ref[...]
    row8 = lax.broadcasted_iota(jnp.int32, (8, 1), 0)
    tail = jnp.where(i % tiles_per_seq != 0, carry_ref[j], 0.0)
    carry_ref[j] = acc[bm - 8:, :]
    out = cb_ref[...] + cw_ref[SSM_CONV - 1:SSM_CONV, :] * acc
    for back in range(1, SSM_CONV):
        rolled = pltpu.roll(acc, back, 0)
        head = jnp.where(row8 < back, pltpu.roll(tail, back, 0), rolled[:8])
        shifted = jnp.concatenate([head, rolled[8:]], axis=0)
        out = out + cw_ref[SSM_CONV - 1 - back:SSM_CONV - back, :] * shifted
    o_ref[...] = _silu(out).astype(BF16)


def _ssm_conv_proj(x, gain, w_in, conv_w, conv_b, *, bm, bn, seq, col0, n):
    t, d = x.shape
    return pl.pallas_call(
        functools.partial(_ssm_conv_kernel, tiles_per_seq=seq // bm),
        out_shape=jax.ShapeDtypeStruct((t, n), BF16),
        grid=(t // bm, n // bn),
        in_specs=[pl.BlockSpec((bm, d), lambda i, j: (i, 0)),
                  pl.BlockSpec((1, d), lambda i, j: (0, 0)),
                  pl.BlockSpec((d, bn), lambda i, j: (0, col0 + j)),
                  pl.BlockSpec((SSM_CONV, bn), lambda i, j: (0, j)),
                  pl.BlockSpec((1, bn), lambda i, j: (0, j))],
        out_specs=pl.BlockSpec((bm, bn), lambda i, j: (i, j)),
        scratch_shapes=[pltpu.VMEM((bm, d), BF16), pltpu.VMEM((bm, bn), F32),
                        pltpu.VMEM((n // bn, 8, bn), F32)],
        compiler_params=_params("arbitrary", "arbitrary"),
        name="ssm_conv_proj",
    )(x, gain[None, :], w_in, conv_w.astype(F32), conv_b.astype(F32)[None, :])


def _ssd_kernel(zs_ref, xs_ref, b_ref, c_ref, acum_t_ref, acum3_ref, dt2_ref, ex3_ref, ex2_ref,
                d_ref, ng_ref, y_ref, state_ref, *, chunk, rpg, gps):
    c = pl.program_id(1)
    L, H, N = chunk, chunk // 2, SSM_STATE
    gw = rpg * HEAD_DIM
    tri = (lax.broadcasted_iota(jnp.int32, (H, H), 0) >= lax.broadcasted_iota(jnp.int32, (H, H), 1))
    acum3 = acum3_ref[...]
    dt2 = dt2_ref[...]
    g0 = pl.program_id(2) * gps

    @pl.when(c == 0)
    def _():
        for u in range(gps):
            state_ref[g0 + u] = jnp.zeros(state_ref.shape[1:], F32)

    for u in range(gps):
        g = g0 + u
        lanes = slice(u * gw, (u + 1) * gw)
        xs = xs_ref[:, lanes].astype(F32)
        bm = b_ref[:, u * N:(u + 1) * N]
        cm = c_ref[:, u * N:(u + 1) * N]

        acum_t = acum_t_ref[u * rpg:(u + 1) * rpg, :]
        acum = _dot(acum3, ex3_ref[u])
        dt_full = _dot(dt2, ex2_ref[u])
        acum_last = acum[L - 1:L, :]

        xdt = xs * dt_full
        xdt_b = xdt.astype(BF16)
        cb = _dot_nt(cm, bm)
        y_parts = []
        for r in range(rpg):
            col = acum[:, r * HEAD_DIM:r * HEAD_DIM + 1]
            row = acum_t[r:r + 1, :]
            d00 = jnp.exp(jnp.where(tri, col[:H] - row[:, :H], -jnp.inf))
            d11 = jnp.exp(jnp.where(tri, col[H:] - row[:, H:], -jnp.inf))
            d10 = jnp.exp(col[H:] - row[:, :H])
            m00 = (cb[:H, :H] * d00).astype(BF16)
            m1 = jnp.concatenate([cb[H:, :H] * d10, cb[H:, H:] * d11], axis=1).astype(BF16)
            x_r = xdt_b[:, r * HEAD_DIM:(r + 1) * HEAD_DIM]
            y_parts.append(jnp.concatenate([_dot(m00, x_r[:H]), _dot(m1, x_r)], axis=0))
        y = jnp.concatenate(y_parts, axis=1)

        state = state_ref[g]
        y = y + _dot(cm, state.astype(BF16)) * jnp.exp(acum)
        to_end = jnp.exp(acum_last - acum)
        state_ref[g] = state * jnp.exp(acum_last) + _dot_tn(bm, (xdt * to_end).astype(BF16))

        y = (y + d_ref[:, lanes] * xs) * zs_ref[:, lanes].astype(F32)
        y = y * lax.rsqrt(jnp.mean(y * y, axis=-1, keepdims=True) + EPS)
        y_ref[:, lanes] = (y * ng_ref[:, lanes]).astype(BF16)


def _ssd(zs, xbc, acum_t, acum3, dt2, d_skip, norm_g, *, batch, gps):
    t, d_inner = zs.shape
    heads = acum_t.shape[0]
    G, N = SSM_GROUPS, SSM_STATE
    rpg = heads // G
    gw = d_inner // G
    L = SSM_CHUNK
    nc = t // batch // L
    b0 = d_inner // (gps * N)
    c0 = b0 + G // gps

    def cur(off):
        return lambda b, c, g: (b * nc + c, off + g)

    per_group = lambda b, c, g: (0, g)
    fixed_rows = lambda b, c, g: (b * nc + c, 0)
    lane_head = jnp.arange(G)[:, None, None] * rpg + jnp.arange(gw)[None, None, :] // HEAD_DIM
    one_hot = (lane_head == jnp.arange(heads)[None, :, None]).astype(BF16)
    rep = lambda v: jnp.repeat(v.astype(F32), HEAD_DIM)[None, :]
    return pl.pallas_call(
        functools.partial(_ssd_kernel, chunk=L, rpg=rpg, gps=gps),
        out_shape=jax.ShapeDtypeStruct((t, d_inner), BF16),
        grid=(batch, nc, G // gps),
        in_specs=[pl.BlockSpec((L, gps * gw), cur(0)),
                  pl.BlockSpec((L, gps * gw), cur(0)),
                  pl.BlockSpec((L, gps * N), cur(b0)),
                  pl.BlockSpec((L, gps * N), cur(c0)),
                  pl.BlockSpec((gps * rpg, L), lambda b, c, g: (g, b * nc + c)),
                  pl.BlockSpec((L, 3 * heads), fixed_rows),
                  pl.BlockSpec((L, 2 * heads), fixed_rows),
                  pl.BlockSpec((gps, 3 * heads, gw), lambda b, c, g: (g, 0, 0)),
                  pl.BlockSpec((gps, 2 * heads, gw), lambda b, c, g: (g, 0, 0)),
                  pl.BlockSpec((1, gps * gw), per_group),
                  pl.BlockSpec((1, gps * gw), per_group)],
        out_specs=pl.BlockSpec((L, gps * gw), cur(0)),
        scratch_shapes=[pltpu.VMEM((G, N, gw), F32)],
        compiler_params=_params("arbitrary", "arbitrary", "arbitrary"),
        name="ssd_scan",
    )(zs, xbc, xbc, xbc, acum_t, acum3, dt2, jnp.tile(one_hot, (1, 3, 1)), jnp.tile(one_hot, (1, 2, 1)),
      rep(d_skip), norm_g.astype(F32)[None, :])


def kernel(x, positions, mixer_norm, ffn_norm, attn_w_qkv, attn_q_norm, attn_k_norm, attn_sinks,
           attn_w_o, ssm_w_in, ssm_conv_w, ssm_conv_b, ssm_dt_bias, ssm_a_log, ssm_d, ssm_norm,
           ssm_w_out, ffn_w_gate, ffn_w_up, ffn_w_down):
    batch, seq, d = x.shape
    t = batch * seq
    xf = x.reshape(t, d)
    pos = positions.reshape(t, 1)
    d_inner = ssm_w_out.shape[1]
    main_w = 2 * d_inner + 2 * SSM_GROUPS * SSM_STATE
    bm = _pick(t, (512, 256, 128))
    bm_big = _pick(seq, (1024, 512, 256, 128))
    bf = _pick(ffn_w_gate.shape[2], (512, 256))
    bn = _pick(d, (1024, 512))

    w_gate, w_up, w_down = (w.astype(BF16) for w in (ffn_w_gate, ffn_w_up, ffn_w_down))

    def ffn(xf, i):
        return _ffn(xf, ffn_norm[i], w_gate, w_up, w_down, layer=i, bm=bm_big, bf=bf)

    q, k, v = _qkv_proj(xf, pos, mixer_norm[0], attn_w_qkv[0].astype(BF16),
                        attn_q_norm[0], attn_k_norm[0], bm=bm)
    att = _attention(q, k, v, attn_sinks[0], batch=batch)
    xf = _proj_residual(xf, att, attn_w_o[0].astype(BF16), bm=bm_big, bn=bn)
    xf = ffn(xf, 0)

    w_in = ssm_w_in[0].astype(BF16)
    bn_in = _pick(SSM_GROUPS * SSM_STATE, (1024, 512, 256))
    zs, acum_t, acum3, dt2 = _ssm_gate_proj(
        xf, mixer_norm[1], w_in, w_in[:, main_w:], ssm_dt_bias[0], ssm_a_log[0],
        bm=bm_big, bn=bn_in, n=d_inner)
    xbc = _ssm_conv_proj(xf, mixer_norm[1], w_in, ssm_conv_w[0], ssm_conv_b[0],
                         bm=bm_big, bn=bn_in, seq=seq, col0=d_inner // bn_in, n=main_w - d_inner)
    y = _ssd(zs, xbc, acum_t, acum3, dt2, ssm_d[0], ssm_norm[0], batch=batch, gps=2)
    xf = _proj_residual(xf, y, ssm_w_out[0].astype(BF16), bm=bm, bn=bn)
    xf = ffn(xf, 1)
    return xf.reshape(batch, seq, d)
```

```python
import functools

import jax
import jax.numpy as jnp
from jax import lax
from jax.experimental import pallas as pl
from jax.experimental.pallas import tpu as pltpu

EPS = 1e-6
HEAD_DIM = 64
ATT_GROUP = 8
WINDOW = 128
ROPE_THETA = 10000.0
SSM_GROUPS = 8
SSM_STATE = 128
SSM_CONV = 4
SSM_CHUNK = 256
LANES = 128
VMEM_LIMIT_BYTES = 56 * 1024 * 1024

F32 = jnp.float32
BF16 = jnp.bfloat16


def _params(*sem):
    return pltpu.CompilerParams(dimension_semantics=sem, vmem_limit_bytes=VMEM_LIMIT_BYTES)


def _pick(n, candidates):
    for c in candidates:
        if n % c == 0:
            return c
    raise ValueError(f"no tile in {candidates} divides {n}")


def _rms_norm_bf16(x, gain):
    y = x * lax.rsqrt(jnp.mean(x * x, axis=-1, keepdims=True) + EPS)
    return (y * gain).astype(BF16)


def _dot(a, b):
    return jnp.dot(a, b, preferred_element_type=F32)


def _dot_nt(a, b):
    return lax.dot_general(a, b, (((1,), (1,)), ((), ())), preferred_element_type=F32)


def _dot_tn(a, b):
    return lax.dot_general(a, b, (((0,), (0,)), ((), ())), preferred_element_type=F32)


def _qkv_kernel(x_ref, pos_ref, g_ref, w_ref, invf_ref, qg_ref, kg_ref,
                q_ref, k_ref, v_ref, acc_ref, *, dq, dk, cw):
    h = _rms_norm_bf16(x_ref[...], g_ref[...])
    ang = pos_ref[...].astype(F32) * invf_ref[...]
    lane = lax.broadcasted_iota(jnp.int32, (1, LANES), 1)
    first_half = (lane % HEAD_DIM) < (HEAD_DIM // 2)
    cos = jnp.cos(ang)
    sin = jnp.sin(ang)
    sin_signed = jnp.where(first_half, -sin, sin)

    r_i = lax.broadcasted_iota(jnp.int32, (cw, cw), 0)
    c_i = lax.broadcasted_iota(jnp.int32, (cw, cw), 1)
    same_head = (r_i // HEAD_DIM == c_i // HEAD_DIM).astype(BF16)

    def norm_rope(a, gain, scale):
        sq = a * a
        sq_hi = sq.astype(BF16)
        sq_lo = (sq - sq_hi.astype(F32)).astype(BF16)
        ss = _dot(sq_hi, same_head) + _dot(sq_lo, same_head)
        outs = []
        for s0 in range(0, cw, LANES):
            sl = slice(s0, s0 + LANES)
            y = a[:, sl] * lax.rsqrt(ss[:, sl] * (1.0 / HEAD_DIM) + EPS) * gain
            partner = jnp.where(first_half,
                                pltpu.roll(y, LANES - HEAD_DIM // 2, 1),
                                pltpu.roll(y, HEAD_DIM // 2, 1))
            outs.append(((y * cos + partner * sin_signed) * scale).astype(BF16))
        return jnp.concatenate(outs, axis=1)

    q_scale = HEAD_DIM ** -0.5
    acc_ref[...] = _dot(h, w_ref[:, :dq + dk])
    v_ref[...] = _dot(h, w_ref[:, dq + dk:]).astype(BF16)
    for c0 in range(0, dq + dk, cw):
        acc = acc_ref[:, c0:c0 + cw]
        if c0 < dq:
            q_ref[:, c0:c0 + cw] = norm_rope(acc, qg_ref[...], q_scale)
        else:
            k_ref[:, c0 - dq:c0 - dq + cw] = norm_rope(acc, kg_ref[...], 1.0)


def _qkv_proj(x, pos, gain, w, q_gain, k_gain, *, bm):
    t, d = x.shape
    dq = d
    dk = d // ATT_GROUP
    cw = _pick(dk, (256, 128))
    inv_freq = ROPE_THETA ** (-jnp.arange(0, HEAD_DIM, 2, dtype=F32) / HEAD_DIM)
    invf = jnp.tile(inv_freq, LANES // (HEAD_DIM // 2))[None, :]
    tile_gain = lambda g: jnp.tile(g.astype(F32), LANES // HEAD_DIM)[None, :]
    row = lambda i: (i, 0)
    fixed = lambda i: (0, 0)
    return pl.pallas_call(
        functools.partial(_qkv_kernel, dq=dq, dk=dk, cw=cw),
        out_shape=(jax.ShapeDtypeStruct((t, dq), BF16),
                   jax.ShapeDtypeStruct((t, dk), BF16),
                   jax.ShapeDtypeStruct((t, dk), BF16)),
        grid=(t // bm,),
        in_specs=[pl.BlockSpec((bm, d), row),
                  pl.BlockSpec((bm, 1), row),
                  pl.BlockSpec((1, d), fixed),
                  pl.BlockSpec((d, dq + 2 * dk), fixed),
                  pl.BlockSpec((1, LANES), fixed),
                  pl.BlockSpec((1, LANES), fixed),
                  pl.BlockSpec((1, LANES), fixed)],
        out_specs=(pl.BlockSpec((bm, dq), row),
                   pl.BlockSpec((bm, dk), row),
                   pl.BlockSpec((bm, dk), row)),
        scratch_shapes=[pltpu.VMEM((bm, dq + dk), F32)],
        compiler_params=_params("arbitrary"),
        name="qkv_proj",
    )(x, pos, gain[None, :], w, invf, tile_gain(q_gain), tile_gain(k_gain))


def _attn_kernel(sink_ref, q_ref, kp_ref, ko_ref, vp_ref, vo_ref, o_ref, s_ref, *, n_kv):
    n = pl.program_id(1)
    W, pairs, slab = WINDOW, ATT_GROUP // 2, 2 * HEAD_DIM
    qi = lax.broadcasted_iota(jnp.int32, (W, 2 * W), 0)
    kj = lax.broadcasted_iota(jnp.int32, (W, 2 * W), 1)
    lag = qi + W - kj
    first_key = jnp.where(n > 0, 0, W)
    valid = ((lag >= 0) & (lag < W) & (kj >= first_key))[None]
    sink_col = kj[:1] == 0
    key_row = lax.broadcasted_iota(jnp.int32, (2 * W, 1), 0)
    zeros = jnp.zeros((2 * W, HEAD_DIM), BF16)
    ones = jnp.ones((2 * W, HEAD_DIM), BF16)
    v_wide = []
    for kv in range(n_kv):
        ks = slice(kv * HEAD_DIM, (kv + 1) * HEAD_DIM)
        k = jnp.concatenate([kp_ref[:, ks], ko_ref[:, ks]], axis=0)
        v = jnp.concatenate([vp_ref[:, ks], vo_ref[:, ks]], axis=0)
        v = jnp.where(key_row == 0, jnp.zeros_like(v), v)
        v_wide.append((jnp.concatenate([v, zeros, ones, zeros], axis=1),
                       jnp.concatenate([zeros, v, zeros, ones], axis=1)))
        c0 = kv * ATT_GROUP * HEAD_DIM
        qp = jnp.concatenate([q_ref[:, c0 + p * slab:c0 + (p + 1) * slab] for p in range(pairs)],
                             axis=0)
        s_ref[2 * kv] = _dot_nt(qp, jnp.concatenate([k, zeros], axis=1))
        s_ref[2 * kv + 1] = _dot_nt(qp, jnp.concatenate([zeros, k], axis=1))

    for kv in range(n_kv):
        c0 = kv * ATT_GROUP * HEAD_DIM

        def probs(parity):
            sink = jnp.concatenate(
                [jnp.full((1, 1, 1), sink_ref[kv * ATT_GROUP + 2 * p + parity], F32)
                 for p in range(pairs)], axis=0)
            fill = jnp.where(sink_col[None], sink, -jnp.inf)
            s = jnp.where(valid, s_ref[2 * kv + parity].reshape(pairs, W, 2 * W), fill)
            m = jnp.max(s, axis=-1, keepdims=True)
            return jnp.exp(s - m).astype(BF16).reshape(pairs * W, 2 * W)

        nd = _dot(probs(0), v_wide[kv][0]) + _dot(probs(1), v_wide[kv][1])
        out = (nd[:, :slab] / nd[:, slab:]).astype(BF16)
        for p in range(pairs):
            o_ref[:, c0 + p * slab:c0 + (p + 1) * slab] = out[p * W:(p + 1) * W]


def _attention(q, k, v, sinks, *, batch):
    t, dq = q.shape
    dk = k.shape[1]
    nb = t // batch // WINDOW
    own = lambda b, n: (b * nb + n, 0)
    prev = lambda b, n: (b * nb + jnp.maximum(n - 1, 0), 0)
    return pl.pallas_call(
        functools.partial(_attn_kernel, n_kv=dk // HEAD_DIM),
        out_shape=jax.ShapeDtypeStruct((t, dq), BF16),
        grid=(batch, nb),
        in_specs=[pl.BlockSpec(memory_space=pltpu.SMEM),
                  pl.BlockSpec((WINDOW, dq), own),
                  pl.BlockSpec((WINDOW, dk), prev),
                  pl.BlockSpec((WINDOW, dk), own),
                  pl.BlockSpec((WINDOW, dk), prev),
                  pl.BlockSpec((WINDOW, dk), own)],
        out_specs=pl.BlockSpec((WINDOW, dq), own),
        scratch_shapes=[pltpu.VMEM((2 * (dk // HEAD_DIM), ATT_GROUP // 2 * WINDOW, 2 * WINDOW), F32)],
        compiler_params=_params("arbitrary", "arbitrary"),
        name="swa_attention",
    )(sinks.astype(F32), q, k, k, v, v)


def _proj_residual_kernel(x_ref, a_ref, w_ref, o_ref):
    o_ref[...] = x_ref[...] + _dot(a_ref[...], w_ref[...])


def _proj_residual(x, a, w, *, bm, bn):
    t, d = x.shape
    k = a.shape[1]
    return pl.pallas_call(
        _proj_residual_kernel,
        out_shape=jax.ShapeDtypeStruct((t, d), F32),
        grid=(d // bn, t // bm),
        in_specs=[pl.BlockSpec((bm, bn), lambda j, i: (i, j)),
                  pl.BlockSpec((bm, k), lambda j, i: (i, 0)),
                  pl.BlockSpec((k, bn), lambda j, i: (0, j))],
        out_specs=pl.BlockSpec((bm, bn), lambda j, i: (i, j)),
        compiler_params=_params("arbitrary", "arbitrary"),
        name="proj_residual",
    )(x, a, w)


def _ffn_kernel(x_ref, g_ref, wg_ref, wu_ref, wd_ref, o_ref, h_ref):
    f = pl.program_id(1)

    @pl.when(f == 0)
    def _():
        x = x_ref[...]
        h_ref[...] = _rms_norm_bf16(x, g_ref[...])
        o_ref[...] = x

    h = h_ref[...]
    gate = _dot(h, wg_ref[...])
    up = _dot(h, wu_ref[...])
    act = (gate * jax.nn.sigmoid(gate) * up).astype(BF16)
    o_ref[...] += _dot(act, wd_ref[...])


def _ffn(x, gain, w_gate, w_up, w_down, *, layer, bm, bf):
    t, d = x.shape
    dff = w_gate.shape[2]
    return pl.pallas_call(
        _ffn_kernel,
        out_shape=jax.ShapeDtypeStruct((t, d), F32),
        grid=(t // bm, dff // bf),
        in_specs=[pl.BlockSpec((bm, d), lambda i, f: (i, 0)),
                  pl.BlockSpec((1, d), lambda i, f: (0, 0)),
                  pl.BlockSpec((None, d, bf), lambda i, f: (layer, 0, f)),
                  pl.BlockSpec((None, d, bf), lambda i, f: (layer, 0, f)),
                  pl.BlockSpec((None, bf, d), lambda i, f: (layer, f, 0))],
        out_specs=pl.BlockSpec((bm, d), lambda i, f: (i, 0)),
        scratch_shapes=[pltpu.VMEM((bm, d), BF16)],
        compiler_params=_params("arbitrary", "arbitrary"),
        name="swiglu_ffn",
    )(x, gain[None, :], w_gate, w_up, w_down)


def _softplus(x):
    return jnp.maximum(x, 0.0) + jnp.log1p(jnp.exp(-jnp.abs(x)))


def _silu(x):
    return x * jax.nn.sigmoid(x)


def _bf16_terms(x, n):
    terms = []
    for _ in range(n):
        t = x.astype(BF16).astype(F32)
        terms.append(t)
        x = x - t
    return terms


def _ssm_gate_kernel(x_ref, g_ref, w_ref, wdt_ref, wdtt_ref, dtb_row_ref, dtb_col_ref,
                     alog_row_ref, alog_col_ref, zs_ref, acum_t_ref, acum3_ref, dt2_ref, h_ref,
                     *, chunk):
    @pl.when(pl.program_id(1) == 0)
    def _():
        h = _rms_norm_bf16(x_ref[...], g_ref[...])
        h_ref[...] = h
        heads = wdtt_ref.shape[0]
        L = chunk
        dt_c = _softplus(_dot(h, wdt_ref[...]) + dtb_row_ref[...])
        dt_t = _softplus(_dot_nt(wdtt_ref[...], h) + dtb_col_ref[...])
        dta_c = dt_c * (-jnp.exp(alog_row_ref[...]))
        dta_t = dt_t * (-jnp.exp(alog_col_ref[...]))
        r_i = lax.broadcasted_iota(jnp.int32, (L, L), 0)
        c_i = lax.broadcasted_iota(jnp.int32, (L, L), 1)
        upper = (r_i <= c_i).astype(BF16)
        lower = (r_i >= c_i).astype(BF16)
        for r0 in range(0, x_ref.shape[0], L):
            rows = slice(r0, r0 + L)
            stacked = jnp.concatenate(_bf16_terms(dta_t[:, rows], 3), axis=0).astype(BF16)
            sums = _dot(stacked, upper)
            acum_t_ref[:, rows] = sums[:heads] + sums[heads:2 * heads] + sums[2 * heads:]
            acum_c = sum(_dot(lower, term.astype(BF16)) for term in _bf16_terms(dta_c[rows], 3))
            acum3_ref[rows, :] = jnp.concatenate(_bf16_terms(acum_c, 3), axis=1).astype(BF16)
        dt2_ref[...] = jnp.concatenate(_bf16_terms(dt_c, 2), axis=1).astype(BF16)

    zs_ref[...] = _silu(_dot(h_ref[...], w_ref[...])).astype(BF16)


def _ssm_gate_proj(x, gain, w_in, w_dt, dt_bias, a_log, *, bm, bn, n):
    t, d = x.shape
    heads = w_dt.shape[1]
    fixed = lambda i, j: (0, 0)
    row_tile = lambda i, j: (i, 0)
    f32 = lambda v: v.astype(F32)
    return pl.pallas_call(
        functools.partial(_ssm_gate_kernel, chunk=SSM_CHUNK),
        out_shape=(jax.ShapeDtypeStruct((t, n), BF16),
                   jax.ShapeDtypeStruct((heads, t), F32),
                   jax.ShapeDtypeStruct((t, 3 * heads), BF16),
                   jax.ShapeDtypeStruct((t, 2 * heads), BF16)),
        grid=(t // bm, n // bn),
        in_specs=[pl.BlockSpec((bm, d), row_tile),
                  pl.BlockSpec((1, d), fixed),
                  pl.BlockSpec((d, bn), lambda i, j: (0, j)),
                  pl.BlockSpec((d, heads), fixed),
                  pl.BlockSpec((heads, d), fixed),
                  pl.BlockSpec((1, heads), fixed),
                  pl.BlockSpec((heads, 1), fixed),
                  pl.BlockSpec((1, heads), fixed),
                  pl.BlockSpec((heads, 1), fixed)],
        out_specs=(pl.BlockSpec((bm, bn), lambda i, j: (i, j)),
                   pl.BlockSpec((heads, bm), lambda i, j: (0, i)),
                   pl.BlockSpec((bm, 3 * heads), row_tile),
                   pl.BlockSpec((bm, 2 * heads), row_tile)),
        scratch_shapes=[pltpu.VMEM((bm, d), BF16)],
        compiler_params=_params("arbitrary", "arbitrary"),
        name="ssm_gate_proj",
    )(x, gain[None, :], w_in, w_dt, w_dt.T, f32(dt_bias)[None, :], f32(dt_bias)[:, None],
      f32(a_log)[None, :], f32(a_log)[:, None])


def _ssm_conv_kernel(x_ref, g_ref, w_ref, cw_ref, cb_ref, o_ref, h_ref, acc_ref, carry_ref,
                     *, tiles_per_seq):
    i = pl.program_id(0)
    j = pl.program_id(1)
    bm = x_ref.shape[0]

    @pl.when(j == 0)
    def _():
        h_ref[...] = _rms_norm_bf16(x_ref[...], g_ref[...])

    acc_ref[...] = _dot(h_ref[...], w_ref[...])
    acc = acc_ref[...]
    row8 = lax.broadcasted_iota(jnp.int32, (8, 1), 0)
    tail = jnp.where(i % tiles_per_seq != 0, carry_ref[j], 0.0)
    carry_ref[j] = acc[bm - 8:, :]
    out = cb_ref[...] + cw_ref[SSM_CONV - 1:SSM_CONV, :] * acc
    for back in range(1, SSM_CONV):
        rolled = pltpu.roll(acc, back, 0)
        head = jnp.where(row8 < back, pltpu.roll(tail, back, 0), rolled[:8])
        shifted = jnp.concatenate([head, rolled[8:]], axis=0)
        out = out + cw_ref[SSM_CONV - 1 - back:SSM_CONV - back, :] * shifted
    o_ref[...] = _silu(out).astype(BF16)


def _ssm_conv_proj(x, gain, w_in, conv_w, conv_b, *, bm, bn, seq, col0, n):
    t, d = x.shape
    return pl.pallas_call(
        functools.partial(_ssm_conv_kernel, tiles_per_seq=seq // bm),
        out_shape=jax.ShapeDtypeStruct((t, n), BF16),
        grid=(t // bm, n // bn),
        in_specs=[pl.BlockSpec((bm, d), lambda i, j: (i, 0)),
                  pl.BlockSpec((1, d), lambda i, j: (0, 0)),
                  pl.BlockSpec((d, bn), lambda i, j: (0, col0 + j)),
                  pl.BlockSpec((SSM_CONV, bn), lambda i, j: (0, j)),
                  pl.BlockSpec((1, bn), lambda i, j: (0, j))],
        out_specs=pl.BlockSpec((bm, bn), lambda i, j: (i, j)),
        scratch_shapes=[pltpu.VMEM((bm, d), BF16), pltpu.VMEM((bm, bn), F32),
                        pltpu.VMEM((n // bn, 8, bn), F32)],
        compiler_params=_params("arbitrary", "arbitrary"),
        name="ssm_conv_proj",
    )(x, gain[None, :], w_in, conv_w.astype(F32), conv_b.astype(F32)[None, :])


def _ssd_kernel(zs_ref, xs_ref, b_ref, c_ref, acum_t_ref, acum3_ref, dt2_ref, ex3_ref, ex2_ref,
                d_ref, ng_ref, y_ref, state_ref, *, chunk, rpg, gps):
    c = pl.program_id(1)
    L, H, N = chunk, chunk // 2, SSM_STATE
    gw = rpg * HEAD_DIM
    tri = (lax.broadcasted_iota(jnp.int32, (H, H), 0) >= lax.broadcasted_iota(jnp.int32, (H, H), 1))
    acum3 = acum3_ref[...]
    dt2 = dt2_ref[...]
    g0 = pl.program_id(2) * gps

    @pl.when(c == 0)
    def _():
        for u in range(gps):
            state_ref[g0 + u] = jnp.zeros(state_ref.shape[1:], F32)

    for u in range(gps):
        g = g0 + u
        lanes = slice(u * gw, (u + 1) * gw)
        xs = xs_ref[:, lanes].astype(F32)
        bm = b_ref[:, u * N:(u + 1) * N]
        cm = c_ref[:, u * N:(u + 1) * N]

        acum_t = acum_t_ref[u * rpg:(u + 1) * rpg, :]
        acum = _dot(acum3, ex3_ref[u])
        dt_full = _dot(dt2, ex2_ref[u])
        acum_last = acum[L - 1:L, :]

        xdt = xs * dt_full
        xdt_b = xdt.astype(BF16)
        cb = _dot_nt(cm, bm)
        y_parts = []
        for r in range(rpg):
            col = acum[:, r * HEAD_DIM:r * HEAD_DIM + 1]
            row = acum_t[r:r + 1, :]
            d00 = jnp.exp(jnp.where(tri, col[:H] - row[:, :H], -jnp.inf))
            d11 = jnp.exp(jnp.where(tri, col[H:] - row[:, H:], -jnp.inf))
            d10 = jnp.exp(col[H:] - row[:, :H])
            m00 = (cb[:H, :H] * d00).astype(BF16)
            m1 = jnp.concatenate([cb[H:, :H] * d10, cb[H:, H:] * d11], axis=1).astype(BF16)
            x_r = xdt_b[:, r * HEAD_DIM:(r + 1) * HEAD_DIM]
            y_parts.append(jnp.concatenate([_dot(m00, x_r[:H]), _dot(m1, x_r)], axis=0))
        y = jnp.concatenate(y_parts, axis=1)

        state = state_ref[g]
        y = y + _dot(cm, state.astype(BF16)) * jnp.exp(acum)
        to_end = jnp.exp(acum_last - acum)
        state_ref[g] = state * jnp.exp(acum_last) + _dot_tn(bm, (xdt * to_end).astype(BF16))

        y = (y + d_ref[:, lanes] * xs) * zs_ref[:, lanes].astype(F32)
        y = y * lax.rsqrt(jnp.mean(y * y, axis=-1, keepdims=True) + EPS)
        y_ref[:, lanes] = (y * ng_ref[:, lanes]).astype(BF16)


def _ssd(zs, xbc, acum_t, acum3, dt2, d_skip, norm_g, *, batch, gps):
    t, d_inner = zs.shape
    heads = acum_t.shape[0]
    G, N = SSM_GROUPS, SSM_STATE
    rpg = heads // G
    gw = d_inner // G
    L = SSM_CHUNK
    nc = t // batch // L
    b0 = d_inner // (gps * N)
    c0 = b0 + G // gps

    def cur(off):
        return lambda b, c, g: (b * nc + c, off + g)

    per_group = lambda b, c, g: (0, g)
    fixed_rows = lambda b, c, g: (b * nc + c, 0)
    lane_head = jnp.arange(G)[:, None, None] * rpg + jnp.arange(gw)[None, None, :] // HEAD_DIM
    one_hot = (lane_head == jnp.arange(heads)[None, :, None]).astype(BF16)
    rep = lambda v: jnp.repeat(v.astype(F32), HEAD_DIM)[None, :]
    return pl.pallas_call(
        functools.partial(_ssd_kernel, chunk=L, rpg=rpg, gps=gps),
        out_shape=jax.ShapeDtypeStruct((t, d_inner), BF16),
        grid=(batch, nc, G // gps),
        in_specs=[pl.BlockSpec((L, gps * gw), cur(0)),
                  pl.BlockSpec((L, gps * gw), cur(0)),
                  pl.BlockSpec((L, gps * N), cur(b0)),
                  pl.BlockSpec((L, gps * N), cur(c0)),
                  pl.BlockSpec((gps * rpg, L), lambda b, c, g: (g, b * nc + c)),
                  pl.BlockSpec((L, 3 * heads), fixed_rows),
                  pl.BlockSpec((L, 2 * heads), fixed_rows),
                  pl.BlockSpec((gps, 3 * heads, gw), lambda b, c, g: (g, 0, 0)),
                  pl.BlockSpec((gps, 2 * heads, gw), lambda b, c, g: (g, 0, 0)),
                  pl.BlockSpec((1, gps * gw), per_group),
                  pl.BlockSpec((1, gps * gw), per_group)],
        out_specs=pl.BlockSpec((L, gps * gw), cur(0)),
        scratch_shapes=[pltpu.VMEM((G, N, gw), F32)],
        compiler_params=_params("arbitrary", "arbitrary", "arbitrary"),
        name="ssd_scan",
    )(zs, xbc, xbc, xbc, acum_t, acum3, dt2, jnp.tile(one_hot, (1, 3, 1)), jnp.tile(one_hot, (1, 2, 1)),
      rep(d_skip), norm_g.astype(F32)[None, :])


def kernel(x, positions, mixer_norm, ffn_norm, attn_w_qkv, attn_q_norm, attn_k_norm, attn_sinks,
           attn_w_o, ssm_w_in, ssm_conv_w, ssm_conv_b, ssm_dt_bias, ssm_a_log, ssm_d, ssm_norm,
           ssm_w_out, ffn_w_gate, ffn_w_up, ffn_w_down):
    batch, seq, d = x.shape
    t = batch * seq
    xf = x.reshape(t, d)
    pos = positions.reshape(t, 1)
    d_inner = ssm_w_out.shape[1]
    main_w = 2 * d_inner + 2 * SSM_GROUPS * SSM_STATE
    bm = _pick(t, (512, 256, 128))
    bm_big = _pick(seq, (1024, 512, 256, 128))
    bf = _pick(ffn_w_gate.shape[2], (512, 256))
    bn = _pick(d, (1024, 512))

    w_gate, w_up, w_down = (w.astype(BF16) for w in (ffn_w_gate, ffn_w_up, ffn_w_down))

    def ffn(xf, i):
        return _ffn(xf, ffn_norm[i], w_gate, w_up, w_down, layer=i, bm=bm_big, bf=bf)

    q, k, v = _qkv_proj(xf, pos, mixer_norm[0], attn_w_qkv[0].astype(BF16),
                        attn_q_norm[0], attn_k_norm[0], bm=bm)
    att = _attention(q, k, v, attn_sinks[0], batch=batch)
    xf = _proj_residual(xf, att, attn_w_o[0].astype(BF16), bm=bm_big, bn=bn)
    xf = ffn(xf, 0)

    w_in = ssm_w_in[0].astype(BF16)
    bn_in = _pick(SSM_GROUPS * SSM_STATE, (1024, 512, 256))
    zs, acum_t, acum3, dt2 = _ssm_gate_proj(
        xf, mixer_norm[1], w_in, w_in[:, main_w:], ssm_dt_bias[0], ssm_a_log[0],
        bm=bm_big, bn=bn_in, n=d_inner)
    xbc = _ssm_conv_proj(xf, mixer_norm[1], w_in, ssm_conv_w[0], ssm_conv_b[0],
                         bm=bm_big, bn=bn_in, seq=seq, col0=d_inner // bn_in, n=main_w - d_inner)
    y = _ssd(zs, xbc, acum_t, acum3, dt2, ssm_d[0], ssm_norm[0], batch=batch, gps=4)
    xf = _proj_residual(xf, y, ssm_w_out[0].astype(BF16), bm=bm, bn=bn)
    xf = ffn(xf, 1)
    return xf.reshape(batch, seq, d)
```

```python
import functools

import jax
import jax.numpy as jnp
from jax import lax
from jax.experimental import pallas as pl
from jax.experimental.pallas import tpu as pltpu

EPS = 1e-6
HEAD_DIM = 64
ATT_GROUP = 8
WINDOW = 128
ROPE_THETA = 10000.0
SSM_GROUPS = 8
SSM_STATE = 128
SSM_CONV = 4
SSM_CHUNK = 256
LOG2_E = 1.4426950408889634
LANES = 128
VMEM_LIMIT_BYTES = 56 * 1024 * 1024

F32 = jnp.float32
BF16 = jnp.bfloat16


def _params(*sem):
    return pltpu.CompilerParams(dimension_semantics=sem, vmem_limit_bytes=VMEM_LIMIT_BYTES)


def _pick(n, candidates):
    for c in candidates:
        if n % c == 0:
            return c
    raise ValueError(f"no tile in {candidates} divides {n}")


def _ride_along_cast(w, steps, step_index):
    layers, rows, cols = w.shape
    slab = layers * rows // steps
    per_layer = rows // slab
    assert slab * steps == layers * rows and per_layer * slab == rows and slab % 16 == 0
    imap = lambda *g: (step_index(*g) // per_layer, step_index(*g) % per_layer, 0)
    spec = pl.BlockSpec((None, slab, cols), imap)
    return spec, jax.ShapeDtypeStruct(w.shape, BF16)


def _rms_norm_bf16(x, gain):
    y = x * lax.rsqrt(jnp.mean(x * x, axis=-1, keepdims=True) + EPS)
    return (y * gain).astype(BF16)


def _dot(a, b):
    return jnp.dot(a, b, preferred_element_type=F32)


def _dot_nt(a, b):
    return lax.dot_general(a, b, (((1,), (1,)), ((), ())), preferred_element_type=F32)


def _dot_tn(a, b):
    return lax.dot_general(a, b, (((0,), (0,)), ((), ())), preferred_element_type=F32)


def _qkv_kernel(x_ref, pos_ref, g_ref, w_ref, invf_ref, qg_ref, kg_ref, side_ref,
                q_ref, k_ref, v_ref, side16_ref, acc_ref, *, dq, dk, cw):
    side16_ref[...] = side_ref[...].astype(BF16)
    h = _rms_norm_bf16(x_ref[...], g_ref[...])
    ang = pos_ref[...].astype(F32) * invf_ref[...]
    lane = lax.broadcasted_iota(jnp.int32, (1, LANES), 1)
    first_half = (lane % HEAD_DIM) < (HEAD_DIM // 2)
    cos = jnp.cos(ang)
    sin = jnp.sin(ang)
    sin_signed = jnp.where(first_half, -sin, sin)

    r_i = lax.broadcasted_iota(jnp.int32, (cw, cw), 0)
    c_i = lax.broadcasted_iota(jnp.int32, (cw, cw), 1)
    same_head = (r_i // HEAD_DIM == c_i // HEAD_DIM).astype(BF16)

    def norm_rope(a, gain, scale):
        sq = a * a
        sq_hi = sq.astype(BF16)
        sq_lo = (sq - sq_hi.astype(F32)).astype(BF16)
        ss = _dot(sq_hi, same_head) + _dot(sq_lo, same_head)
        outs = []
        for s0 in range(0, cw, LANES):
            sl = slice(s0, s0 + LANES)
            y = a[:, sl] * lax.rsqrt(ss[:, sl] * (1.0 / HEAD_DIM) + EPS) * gain
            partner = jnp.where(first_half,
                                pltpu.roll(y, LANES - HEAD_DIM // 2, 1),
                                pltpu.roll(y, HEAD_DIM // 2, 1))
            outs.append(((y * cos + partner * sin_signed) * scale).astype(BF16))
        return jnp.concatenate(outs, axis=1)

    q_scale = HEAD_DIM ** -0.5
    acc_ref[...] = _dot(h, w_ref[:, :dq + dk])
    v_ref[...] = _dot(h, w_ref[:, dq + dk:]).astype(BF16)
    for c0 in range(0, dq + dk, cw):
        acc = acc_ref[:, c0:c0 + cw]
        if c0 < dq:
            q_ref[:, c0:c0 + cw] = norm_rope(acc, qg_ref[...], q_scale)
        else:
            k_ref[:, c0 - dq:c0 - dq + cw] = norm_rope(acc, kg_ref[...], 1.0)


def _qkv_proj(x, pos, gain, w, q_gain, k_gain, side, *, bm):
    t, d = x.shape
    dq = d
    dk = d // ATT_GROUP
    cw = _pick(dk, (256, 128))
    inv_freq = ROPE_THETA ** (-jnp.arange(0, HEAD_DIM, 2, dtype=F32) / HEAD_DIM)
    invf = jnp.tile(inv_freq, LANES // (HEAD_DIM // 2))[None, :]
    tile_gain = lambda g: jnp.tile(g.astype(F32), LANES // HEAD_DIM)[None, :]
    row = lambda i: (i, 0)
    fixed = lambda i: (0, 0)
    side_spec, side_shape = _ride_along_cast(side, t // bm, lambda i: i)
    return pl.pallas_call(
        functools.partial(_qkv_kernel, dq=dq, dk=dk, cw=cw),
        out_shape=(jax.ShapeDtypeStruct((t, dq), BF16),
                   jax.ShapeDtypeStruct((t, dk), BF16),
                   jax.ShapeDtypeStruct((t, dk), BF16),
                   side_shape),
        grid=(t // bm,),
        in_specs=[pl.BlockSpec((bm, d), row),
                  pl.BlockSpec((bm, 1), row),
                  pl.BlockSpec((1, d), fixed),
                  pl.BlockSpec((d, dq + 2 * dk), fixed),
                  pl.BlockSpec((1, LANES), fixed),
                  pl.BlockSpec((1, LANES), fixed),
                  pl.BlockSpec((1, LANES), fixed),
                  side_spec],
        out_specs=(pl.BlockSpec((bm, dq), row),
                   pl.BlockSpec((bm, dk), row),
                   pl.BlockSpec((bm, dk), row),
                   side_spec),
        scratch_shapes=[pltpu.VMEM((bm, dq + dk), F32)],
        compiler_params=_params("arbitrary"),
        name="qkv_proj",
    )(x, pos, gain[None, :], w, invf, tile_gain(q_gain), tile_gain(k_gain), side)


def _attn_kernel(sink_ref, q_ref, kp_ref, ko_ref, vp_ref, vo_ref, side_a_ref, side_b_ref,
                 o_ref, side_a16_ref, side_b16_ref, s_ref, *, n_kv):
    side_a16_ref[...] = side_a_ref[...].astype(BF16)
    side_b16_ref[...] = side_b_ref[...].astype(BF16)
    n = pl.program_id(1)
    W, pairs, slab = WINDOW, ATT_GROUP // 2, 2 * HEAD_DIM
    qi = lax.broadcasted_iota(jnp.int32, (W, 2 * W), 0)
    kj = lax.broadcasted_iota(jnp.int32, (W, 2 * W), 1)
    lag = qi + W - kj
    first_key = jnp.where(n > 0, 0, W)
    valid = ((lag >= 0) & (lag < W) & (kj >= first_key))[None]
    sink_col = kj[:1] == 0
    key_row = lax.broadcasted_iota(jnp.int32, (2 * W, 1), 0)
    zeros = jnp.zeros((2 * W, HEAD_DIM), BF16)
    ones = jnp.ones((2 * W, HEAD_DIM), BF16)
    v_wide = []
    for kv in range(n_kv):
        ks = slice(kv * HEAD_DIM, (kv + 1) * HEAD_DIM)
        k = jnp.concatenate([kp_ref[:, ks], ko_ref[:, ks]], axis=0)
        v = jnp.concatenate([vp_ref[:, ks], vo_ref[:, ks]], axis=0)
        v = jnp.where(key_row == 0, jnp.zeros_like(v), v)
        v_wide.append((jnp.concatenate([v, zeros, ones, zeros], axis=1),
                       jnp.concatenate([zeros, v, zeros, ones], axis=1)))
        c0 = kv * ATT_GROUP * HEAD_DIM
        qp = jnp.concatenate([q_ref[:, c0 + p * slab:c0 + (p + 1) * slab] for p in range(pairs)],
                             axis=0)
        s_ref[2 * kv] = _dot_nt(qp, jnp.concatenate([k, zeros], axis=1))
        s_ref[2 * kv + 1] = _dot_nt(qp, jnp.concatenate([zeros, k], axis=1))

    for kv in range(n_kv):
        c0 = kv * ATT_GROUP * HEAD_DIM

        def probs(parity):
            sink = jnp.concatenate(
                [jnp.full((1, 1, 1), sink_ref[kv * ATT_GROUP + 2 * p + parity], F32)
                 for p in range(pairs)], axis=0)
            fill = jnp.where(sink_col[None], sink, -jnp.inf)
            s = jnp.where(valid, s_ref[2 * kv + parity].reshape(pairs, W, 2 * W), fill)
            m = jnp.max(s, axis=-1, keepdims=True)
            return jnp.exp(s - m).astype(BF16).reshape(pairs * W, 2 * W)

        nd = _dot(probs(0), v_wide[kv][0]) + _dot(probs(1), v_wide[kv][1])
        out = (nd[:, :slab] / nd[:, slab:]).astype(BF16)
        for p in range(pairs):
            o_ref[:, c0 + p * slab:c0 + (p + 1) * slab] = out[p * W:(p + 1) * W]


def _attention(q, k, v, sinks, side_a, side_b, *, batch):
    t, dq = q.shape
    dk = k.shape[1]
    nb = t // batch // WINDOW
    own = lambda b, n: (b * nb + n, 0)
    prev = lambda b, n: (b * nb + jnp.maximum(n - 1, 0), 0)
    step = lambda b, n: b * nb + n
    spec_a, shape_a = _ride_along_cast(side_a, batch * nb, step)
    spec_b, shape_b = _ride_along_cast(side_b, batch * nb, step)
    return pl.pallas_call(
        functools.partial(_attn_kernel, n_kv=dk // HEAD_DIM),
        out_shape=(jax.ShapeDtypeStruct((t, dq), BF16), shape_a, shape_b),
        grid=(batch, nb),
        in_specs=[pl.BlockSpec(memory_space=pltpu.SMEM),
                  pl.BlockSpec((WINDOW, dq), own),
                  pl.BlockSpec((WINDOW, dk), prev),
                  pl.BlockSpec((WINDOW, dk), own),
                  pl.BlockSpec((WINDOW, dk), prev),
                  pl.BlockSpec((WINDOW, dk), own),
                  spec_a, spec_b],
        out_specs=(pl.BlockSpec((WINDOW, dq), own), spec_a, spec_b),
        scratch_shapes=[pltpu.VMEM((2 * (dk // HEAD_DIM), ATT_GROUP // 2 * WINDOW, 2 * WINDOW), F32)],
        compiler_params=_params("arbitrary", "arbitrary"),
        name="swa_attention",
    )(sinks.astype(F32), q, k, k, v, v, side_a, side_b)


def _proj_residual_kernel(x_ref, a_ref, w_ref, o_ref):
    o_ref[...] = x_ref[...] + _dot(a_ref[...], w_ref[...])


def _proj_residual(x, a, w, *, bm, bn):
    t, d = x.shape
    k = a.shape[1]
    return pl.pallas_call(
        _proj_residual_kernel,
        out_shape=jax.ShapeDtypeStruct((t, d), F32),
        grid=(d // bn, t // bm),
        in_specs=[pl.BlockSpec((bm, bn), lambda j, i: (i, j)),
                  pl.BlockSpec((bm, k), lambda j, i: (i, 0)),
                  pl.BlockSpec((k, bn), lambda j, i: (0, j))],
        out_specs=pl.BlockSpec((bm, bn), lambda j, i: (i, j)),
        compiler_params=_params("arbitrary", "arbitrary"),
        name="proj_residual",
    )(x, a, w)


def _ffn_kernel(x_ref, g_ref, wg_ref, wu_ref, wd_ref, o_ref, h_ref):
    f = pl.program_id(1)

    @pl.when(f == 0)
    def _():
        x = x_ref[...]
        h_ref[...] = _rms_norm_bf16(x, g_ref[...])
        o_ref[...] = x

    h = h_ref[...]
    gate = _dot(h, wg_ref[...])
    up = _dot(h, wu_ref[...])
    act = (gate * jax.nn.sigmoid(gate) * up).astype(BF16)
    o_ref[...] += _dot(act, wd_ref[...])


def _ffn(x, gain, w_gate, w_up, w_down, *, layer, bm, bf):
    t, d = x.shape
    dff = w_gate.shape[2]
    return pl.pallas_call(
        _ffn_kernel,
        out_shape=jax.ShapeDtypeStruct((t, d), F32),
        grid=(t // bm, dff // bf),
        in_specs=[pl.BlockSpec((bm, d), lambda i, f: (i, 0)),
                  pl.BlockSpec((1, d), lambda i, f: (0, 0)),
                  pl.BlockSpec((None, d, bf), lambda i, f: (layer, 0, f)),
                  pl.BlockSpec((None, d, bf), lambda i, f: (layer, 0, f)),
                  pl.BlockSpec((None, bf, d), lambda i, f: (layer, f, 0))],
        out_specs=pl.BlockSpec((bm, d), lambda i, f: (i, 0)),
        scratch_shapes=[pltpu.VMEM((bm, d), BF16)],
        compiler_params=_params("arbitrary", "arbitrary"),
        name="swiglu_ffn",
    )(x, gain[None, :], w_gate, w_up, w_down)


def _softplus(x):
    return jnp.maximum(x, 0.0) + jnp.log1p(jnp.exp(-jnp.abs(x)))


def _silu(x):
    return x * jax.nn.sigmoid(x)


def _bf16_terms(x, n):
    terms = []
    for _ in range(n):
        t = x.astype(BF16).astype(F32)
        terms.append(t)
        x = x - t
    return terms


def _ssm_gate_kernel(x_ref, g_ref, w_ref, wdt_ref, wdtt_ref, dtb_row_ref, dtb_col_ref,
                     alog_row_ref, alog_col_ref, zs_ref, acum_t_ref, acum3_ref, dt2_ref, h_ref,
                     *, chunk):
    @pl.when(pl.program_id(1) == 0)
    def _():
        h = _rms_norm_bf16(x_ref[...], g_ref[...])
        h_ref[...] = h
        heads = wdtt_ref.shape[0]
        L = chunk
        dt_c = _softplus(_dot(h, wdt_ref[...]) + dtb_row_ref[...])
        dt_t = _softplus(_dot_nt(wdtt_ref[...], h) + dtb_col_ref[...])
        dta_c = dt_c * (-LOG2_E * jnp.exp(alog_row_ref[...]))
        dta_t = dt_t * (-LOG2_E * jnp.exp(alog_col_ref[...]))
        r_i = lax.broadcasted_iota(jnp.int32, (L, L), 0)
        c_i = lax.broadcasted_iota(jnp.int32, (L, L), 1)
        upper = (r_i <= c_i).astype(BF16)
        lower = (r_i >= c_i).astype(BF16)
        for r0 in range(0, x_ref.shape[0], L):
            rows = slice(r0, r0 + L)
            stacked = jnp.concatenate(_bf16_terms(dta_t[:, rows], 3), axis=0).astype(BF16)
            sums = _dot(stacked, upper)
            acum_t_ref[:, rows] = sums[:heads] + sums[heads:2 * heads] + sums[2 * heads:]
            acum_c = sum(_dot(lower, term.astype(BF16)) for term in _bf16_terms(dta_c[rows], 3))
            acum3_ref[rows, :] = jnp.concatenate(_bf16_terms(acum_c, 3), axis=1).astype(BF16)
        dt2_ref[...] = jnp.concatenate(_bf16_terms(dt_c, 2), axis=1).astype(BF16)

    zs_ref[...] = _silu(_dot(h_ref[...], w_ref[...])).astype(BF16)


def _ssm_gate_proj(x, gain, w_in, w_dt, dt_bias, a_log, *, bm, bn, n):
    t, d = x.shape
    heads = w_dt.shape[1]
    fixed = lambda i, j: (0, 0)
    row_tile = lambda i, j: (i, 0)
    f32 = lambda v: v.astype(F32)
    return pl.pallas_call(
        functools.partial(_ssm_gate_kernel, chunk=SSM_CHUNK),
        out_shape=(jax.ShapeDtypeStruct((t, n), BF16),
                   jax.ShapeDtypeStruct((heads, t), F32),
                   jax.ShapeDtypeStruct((t, 3 * heads), BF16),
                   jax.ShapeDtypeStruct((t, 2 * heads), BF16)),
        grid=(t // bm, n // bn),
        in_specs=[pl.BlockSpec((bm, d), row_tile),
                  pl.BlockSpec((1, d), fixed),
                  pl.BlockSpec((d, bn), lambda i, j: (0, j)),
                  pl.BlockSpec((d, heads), fixed),
                  pl.BlockSpec((heads, d), fixed),
                  pl.BlockSpec((1, heads), fixed),
                  pl.BlockSpec((heads, 1), fixed),
                  pl.BlockSpec((1, heads), fixed),
                  pl.BlockSpec((heads, 1), fixed)],
        out_specs=(pl.BlockSpec((bm, bn), lambda i, j: (i, j)),
                   pl.BlockSpec((heads, bm), lambda i, j: (0, i)),
                   pl.BlockSpec((bm, 3 * heads), row_tile),
                   pl.BlockSpec((bm, 2 * heads), row_tile)),
        scratch_shapes=[pltpu.VMEM((bm, d), BF16)],
        compiler_params=_params("arbitrary", "arbitrary"),
        name="ssm_gate_proj",
    )(x, gain[None, :], w_in, w_dt, w_dt.T, f32(dt_bias)[None, :], f32(dt_bias)[:, None],
      f32(a_log)[None, :], f32(a_log)[:, None])


def _ssm_conv_kernel(x_ref, g_ref, w_ref, cw_ref, cb_ref, o_ref, h_ref, acc_ref, carry_ref,
                     *, tiles_per_seq):
    i = pl.program_id(0)
    j = pl.program_id(1)
    bm = x_ref.shape[0]

    @pl.when(j == 0)
    def _():
        h_ref[...] = _rms_norm_bf16(x_ref[...], g_ref[...])

    acc_ref[...] = _dot(h_ref[...], w_ref[...])
    acc = acc_ref[...]
    row8 = lax.broadcasted_iota(jnp.int32, (8, 1), 0)
    tail = jnp.where(i % tiles_per_seq != 0, carry_ref[j], 0.0)
    carry_ref[j] = acc[bm - 8:, :]
    out = cb_ref[...] + cw_ref[SSM_CONV - 1:SSM_CONV, :] * acc
    for back in range(1, SSM_CONV):
        rolled = pltpu.roll(acc, back, 0)
        head = jnp.where(row8 < back, pltpu.roll(tail, back, 0), rolled[:8])
        shifted = jnp.concatenate([head, rolled[8:]], axis=0)
        out = out + cw_ref[SSM_CONV - 1 - back:SSM_CONV - back, :] * shifted
    o_ref[...] = _silu(out).astype(BF16)


def _ssm_conv_proj(x, gain, w_in, conv_w, conv_b, *, bm, bn, seq, col0, n):
    t, d = x.shape
    return pl.pallas_call(
        functools.partial(_ssm_conv_kernel, tiles_per_seq=seq // bm),
        out_shape=jax.ShapeDtypeStruct((t, n), BF16),
        grid=(t // bm, n // bn),
        in_specs=[pl.BlockSpec((bm, d), lambda i, j: (i, 0)),
                  pl.BlockSpec((1, d), lambda i, j: (0, 0)),
                  pl.BlockSpec((d, bn), lambda i, j: (0, col0 + j)),
                  pl.BlockSpec((SSM_CONV, bn), lambda i, j: (0, j)),
                  pl.BlockSpec((1, bn), lambda i, j: (0, j))],
        out_specs=pl.BlockSpec((bm, bn), lambda i, j: (i, j)),
        scratch_shapes=[pltpu.VMEM((bm, d), BF16), pltpu.VMEM((bm, bn), F32),
                        pltpu.VMEM((n // bn, 8, bn), F32)],
        compiler_params=_params("arbitrary", "arbitrary"),
        name="ssm_conv_proj",
    )(x, gain[None, :], w_in, conv_w.astype(F32), conv_b.astype(F32)[None, :])


def _ssd_kernel(zs_ref, xs_ref, b_ref, c_ref, acum_t_ref, acum3_ref, dt2_ref, ex3_ref, ex2_ref,
                d_ref, ng_ref, y_ref, state_ref, *, chunk, rpg, gps):
    c = pl.program_id(1)
    L, H, N = chunk, chunk // 2, SSM_STATE
    gw = rpg * HEAD_DIM
    tri = (lax.broadcasted_iota(jnp.int32, (H, H), 0) >= lax.broadcasted_iota(jnp.int32, (H, H), 1))
    acum3 = acum3_ref[...]
    dt2 = dt2_ref[...]
    g0 = pl.program_id(2) * gps

    @pl.when(c == 0)
    def _():
        for u in range(gps):
            state_ref[g0 + u] = jnp.zeros(state_ref.shape[1:], F32)

    for u in range(gps):
        g = g0 + u
        lanes = slice(u * gw, (u + 1) * gw)
        xs = xs_ref[:, lanes].astype(F32)
        bm = b_ref[:, u * N:(u + 1) * N]
        cm = c_ref[:, u * N:(u + 1) * N]

        acum_t = acum_t_ref[u * rpg:(u + 1) * rpg, :]
        acum = _dot(acum3, ex3_ref[u])
        dt_full = _dot(dt2, ex2_ref[u])
        acum_last = acum[L - 1:L, :]

        xdt = xs * dt_full
        xdt_b = xdt.astype(BF16)
        cb = _dot_nt(cm, bm)
        y_parts = []
        for r in range(rpg):
            col = acum[:, r * HEAD_DIM:r * HEAD_DIM + 1]
            row = acum_t[r:r + 1, :]
            d00 = jnp.exp2(jnp.where(tri, col[:H] - row[:, :H], -jnp.inf))
            d11 = jnp.exp2(jnp.where(tri, col[H:] - row[:, H:], -jnp.inf))
            d10 = jnp.exp2(col[H:] - row[:, :H])
            m00 = (cb[:H, :H] * d00).astype(BF16)
            m1 = jnp.concatenate([cb[H:, :H] * d10, cb[H:, H:] * d11], axis=1).astype(BF16)
            x_r = xdt_b[:, r * HEAD_DIM:(r + 1) * HEAD_DIM]
            y_parts.append(jnp.concatenate([_dot(m00, x_r[:H]), _dot(m1, x_r)], axis=0))
        y = jnp.concatenate(y_parts, axis=1)

        state = state_ref[g]
        y = y + _dot(cm, state.astype(BF16)) * jnp.exp2(acum)
        to_end = jnp.exp2(acum_last - acum)
        state_ref[g] = state * jnp.exp2(acum_last) + _dot_tn(bm, (xdt * to_end).astype(BF16))

        y = (y + d_ref[:, lanes] * xs) * zs_ref[:, lanes].astype(F32)
        y = y * lax.rsqrt(jnp.mean(y * y, axis=-1, keepdims=True) + EPS)
        y_ref[:, lanes] = (y * ng_ref[:, lanes]).astype(BF16)


def _ssd(zs, xbc, acum_t, acum3, dt2, d_skip, norm_g, *, batch, gps):
    t, d_inner = zs.shape
    heads = acum_t.shape[0]
    G, N = SSM_GROUPS, SSM_STATE
    rpg = heads // G
    gw = d_inner // G
    L = SSM_CHUNK
    nc = t // batch // L
    b0 = d_inner // (gps * N)
    c0 = b0 + G // gps

    def cur(off):
        return lambda b, c, g: (b * nc + c, off + g)

    per_group = lambda b, c, g: (0, g)
    fixed_rows = lambda b, c, g: (b * nc + c, 0)
    lane_head = jnp.arange(G)[:, None, None] * rpg + jnp.arange(gw)[None, None, :] // HEAD_DIM
    one_hot = (lane_head == jnp.arange(heads)[None, :, None]).astype(BF16)
    rep = lambda v: jnp.repeat(v.astype(F32), HEAD_DIM)[None, :]
    return pl.pallas_call(
        functools.partial(_ssd_kernel, chunk=L, rpg=rpg, gps=gps),
        out_shape=jax.ShapeDtypeStruct((t, d_inner), BF16),
        grid=(batch, nc, G // gps),
        in_specs=[pl.BlockSpec((L, gps * gw), cur(0)),
                  pl.BlockSpec((L, gps * gw), cur(0)),
                  pl.BlockSpec((L, gps * N), cur(b0)),
                  pl.BlockSpec((L, gps * N), cur(c0)),
                  pl.BlockSpec((gps * rpg, L), lambda b, c, g: (g, b * nc + c)),
                  pl.BlockSpec((L, 3 * heads), fixed_rows),
                  pl.BlockSpec((L, 2 * heads), fixed_rows),
                  pl.BlockSpec((gps, 3 * heads, gw), lambda b, c, g: (g, 0, 0)),
                  pl.BlockSpec((gps, 2 * heads, gw), lambda b, c, g: (g, 0, 0)),
                  pl.BlockSpec((1, gps * gw), per_group),
                  pl.BlockSpec((1, gps * gw), per_group)],
        out_specs=pl.BlockSpec((L, gps * gw), cur(0)),
        scratch_shapes=[pltpu.VMEM((G, N, gw), F32)],
        compiler_params=_params("arbitrary", "arbitrary", "arbitrary"),
        name="ssd_scan",
    )(zs, xbc, xbc, xbc, acum_t, acum3, dt2, jnp.tile(one_hot, (1, 3, 1)), jnp.tile(one_hot, (1, 2, 1)),
      rep(d_skip), norm_g.astype(F32)[None, :])


def kernel(x, positions, mixer_norm, ffn_norm, attn_w_qkv, attn_q_norm, attn_k_norm, attn_sinks,
           attn_w_o, ssm_w_in, ssm_conv_w, ssm_conv_b, ssm_dt_bias, ssm_a_log, ssm_d, ssm_norm,
           ssm_w_out, ffn_w_gate, ffn_w_up, ffn_w_down):
    batch, seq, d = x.shape
    t = batch * seq
    xf = x.reshape(t, d)
    pos = positions.reshape(t, 1)
    d_inner = ssm_w_out.shape[1]
    main_w = 2 * d_inner + 2 * SSM_GROUPS * SSM_STATE
    bm = _pick(t, (512, 256, 128))
    bm_big = _pick(seq, (1024, 512, 256, 128))
    bf = _pick(ffn_w_gate.shape[2], (512, 256))
    bn = _pick(d, (1024, 512))

    def ffn(xf, i):
        return _ffn(xf, ffn_norm[i], w_gate, w_up, w_down, layer=i, bm=bm_big, bf=bf)

    q, k, v, w_down = _qkv_proj(xf, pos, mixer_norm[0], attn_w_qkv[0].astype(BF16),
                                attn_q_norm[0], attn_k_norm[0], ffn_w_down, bm=bm)
    att, w_gate, w_up = _attention(q, k, v, attn_sinks[0], ffn_w_gate, ffn_w_up, batch=batch)
    xf = _proj_residual(xf, att, attn_w_o[0].astype(BF16), bm=bm_big, bn=bn)
    xf = ffn(xf, 0)

    w_in = ssm_w_in[0].astype(BF16)
    bn_in = _pick(SSM_GROUPS * SSM_STATE, (1024, 512, 256))
    zs, acum_t, acum3, dt2 = _ssm_gate_proj(
        xf, mixer_norm[1], w_in, w_in[:, main_w:], ssm_dt_bias[0], ssm_a_log[0],
        bm=bm_big, bn=bn_in, n=d_inner)
    xbc = _ssm_conv_proj(xf, mixer_norm[1], w_in, ssm_conv_w[0], ssm_conv_b[0],
                         bm=bm_big, bn=bn_in, seq=seq, col0=d_inner // bn_in, n=main_w - d_inner)
    y = _ssd(zs, xbc, acum_t, acum3, dt2, ssm_d[0], ssm_norm[0], batch=batch, gps=4)
    xf = _proj_residual(xf, y, ssm_w_out[0].astype(BF16), bm=bm, bn=bn)
    xf = ffn(xf, 1)
    return xf.reshape(batch, seq, d)
```

```python
import functools

import jax
import jax.numpy as jnp
from jax import lax
from jax.experimental import pallas as pl
from jax.experimental.pallas import tpu as pltpu

EPS = 1e-6
HEAD_DIM = 64
ATT_GROUP = 8
WINDOW = 128
ROPE_THETA = 10000.0
SSM_GROUPS = 8
SSM_STATE = 128
SSM_CONV = 4
SSM_CHUNK = 256
LOG2_E = 1.4426950408889634
LANES = 128
VMEM_LIMIT_BYTES = 56 * 1024 * 1024

F32 = jnp.float32
BF16 = jnp.bfloat16


def _params(*sem):
    return pltpu.CompilerParams(dimension_semantics=sem, vmem_limit_bytes=VMEM_LIMIT_BYTES)


def _pick(n, candidates):
    for c in candidates:
        if n % c == 0:
            return c
    raise ValueError(f"no tile in {candidates} divides {n}")


def _ride_along_cast(w, steps, step_index):
    layers, rows, cols = w.shape
    slab = layers * rows // steps
    per_layer = rows // slab
    assert slab * steps == layers * rows and per_layer * slab == rows and slab % 16 == 0
    imap = lambda *g: (step_index(*g) // per_layer, step_index(*g) % per_layer, 0)
    spec = pl.BlockSpec((None, slab, cols), imap)
    return spec, jax.ShapeDtypeStruct(w.shape, BF16)


def _rms_norm_bf16(x, gain):
    y = x * lax.rsqrt(jnp.mean(x * x, axis=-1, keepdims=True) + EPS)
    return (y * gain).astype(BF16)


def _dot(a, b):
    return jnp.dot(a, b, preferred_element_type=F32)


def _dot_nt(a, b):
    return lax.dot_general(a, b, (((1,), (1,)), ((), ())), preferred_element_type=F32)


def _dot_tn(a, b):
    return lax.dot_general(a, b, (((0,), (0,)), ((), ())), preferred_element_type=F32)


def _qkv_kernel(x_ref, pos_ref, g_ref, w_ref, invf_ref, qg_ref, kg_ref, side_ref,
                q_ref, k_ref, v_ref, side16_ref, acc_ref, *, dq, dk, cw):
    side16_ref[...] = side_ref[...].astype(BF16)
    h = _rms_norm_bf16(x_ref[...], g_ref[...])
    ang = pos_ref[...].astype(F32) * invf_ref[...]
    lane = lax.broadcasted_iota(jnp.int32, (1, LANES), 1)
    first_half = (lane % HEAD_DIM) < (HEAD_DIM // 2)
    cos = jnp.cos(ang)
    sin = jnp.sin(ang)
    sin_signed = jnp.where(first_half, -sin, sin)

    r_i = lax.broadcasted_iota(jnp.int32, (cw, cw), 0)
    c_i = lax.broadcasted_iota(jnp.int32, (cw, cw), 1)
    same_head = (r_i // HEAD_DIM == c_i // HEAD_DIM).astype(BF16)

    def norm_rope(a, gain, scale):
        sq = a * a
        sq_hi = sq.astype(BF16)
        sq_lo = (sq - sq_hi.astype(F32)).astype(BF16)
        ss = _dot(sq_hi, same_head) + _dot(sq_lo, same_head)
        outs = []
        for s0 in range(0, cw, LANES):
            sl = slice(s0, s0 + LANES)
            y = a[:, sl] * lax.rsqrt(ss[:, sl] * (1.0 / HEAD_DIM) + EPS) * gain
            partner = jnp.where(first_half,
                                pltpu.roll(y, LANES - HEAD_DIM // 2, 1),
                                pltpu.roll(y, HEAD_DIM // 2, 1))
            outs.append(((y * cos + partner * sin_signed) * scale).astype(BF16))
        return jnp.concatenate(outs, axis=1)

    q_scale = HEAD_DIM ** -0.5
    acc_ref[...] = _dot(h, w_ref[:, :dq + dk])
    v_ref[...] = _dot(h, w_ref[:, dq + dk:]).astype(BF16)
    for c0 in range(0, dq + dk, cw):
        acc = acc_ref[:, c0:c0 + cw]
        if c0 < dq:
            q_ref[:, c0:c0 + cw] = norm_rope(acc, qg_ref[...], q_scale)
        else:
            k_ref[:, c0 - dq:c0 - dq + cw] = norm_rope(acc, kg_ref[...], 1.0)


def _qkv_proj(x, pos, gain, w, q_gain, k_gain, side, *, bm):
    t, d = x.shape
    dq = d
    dk = d // ATT_GROUP
    cw = _pick(dk, (256, 128))
    inv_freq = ROPE_THETA ** (-jnp.arange(0, HEAD_DIM, 2, dtype=F32) / HEAD_DIM)
    invf = jnp.tile(inv_freq, LANES // (HEAD_DIM // 2))[None, :]
    tile_gain = lambda g: jnp.tile(g.astype(F32), LANES // HEAD_DIM)[None, :]
    row = lambda i: (i, 0)
    fixed = lambda i: (0, 0)
    side_spec, side_shape = _ride_along_cast(side, t // bm, lambda i: i)
    return pl.pallas_call(
        functools.partial(_qkv_kernel, dq=dq, dk=dk, cw=cw),
        out_shape=(jax.ShapeDtypeStruct((t, dq), BF16),
                   jax.ShapeDtypeStruct((t, dk), BF16),
                   jax.ShapeDtypeStruct((t, dk), BF16),
                   side_shape),
        grid=(t // bm,),
        in_specs=[pl.BlockSpec((bm, d), row),
                  pl.BlockSpec((bm, 1), row),
                  pl.BlockSpec((1, d), fixed),
                  pl.BlockSpec((d, dq + 2 * dk), fixed),
                  pl.BlockSpec((1, LANES), fixed),
                  pl.BlockSpec((1, LANES), fixed),
                  pl.BlockSpec((1, LANES), fixed),
                  side_spec],
        out_specs=(pl.BlockSpec((bm, dq), row),
                   pl.BlockSpec((bm, dk), row),
                   pl.BlockSpec((bm, dk), row),
                   side_spec),
        scratch_shapes=[pltpu.VMEM((bm, dq + dk), F32)],
        compiler_params=_params("arbitrary"),
        name="qkv_proj",
    )(x, pos, gain[None, :], w, invf, tile_gain(q_gain), tile_gain(k_gain), side)


def _attn_kernel(sink_ref, q_ref, kp_ref, ko_ref, vp_ref, vo_ref, side_a_ref, side_b_ref,
                 o_ref, side_a16_ref, side_b16_ref, s_ref, *, n_kv):
    side_a16_ref[...] = side_a_ref[...].astype(BF16)
    side_b16_ref[...] = side_b_ref[...].astype(BF16)
    n = pl.program_id(1)
    W, pairs, slab = WINDOW, ATT_GROUP // 2, 2 * HEAD_DIM
    qi = lax.broadcasted_iota(jnp.int32, (W, 2 * W), 0)
    kj = lax.broadcasted_iota(jnp.int32, (W, 2 * W), 1)
    lag = qi + W - kj
    first_key = jnp.where(n > 0, 0, W)
    valid = ((lag >= 0) & (lag < W) & (kj >= first_key))[None]
    sink_col = kj[:1] == 0
    key_row = lax.broadcasted_iota(jnp.int32, (2 * W, 1), 0)
    zeros = jnp.zeros((2 * W, HEAD_DIM), BF16)
    ones = jnp.ones((2 * W, HEAD_DIM), BF16)
    v_wide = []
    for kv in range(n_kv):
        ks = slice(kv * HEAD_DIM, (kv + 1) * HEAD_DIM)
        k = jnp.concatenate([kp_ref[:, ks], ko_ref[:, ks]], axis=0)
        v = jnp.concatenate([vp_ref[:, ks], vo_ref[:, ks]], axis=0)
        v = jnp.where(key_row == 0, jnp.zeros_like(v), v)
        v_wide.append((jnp.concatenate([v, zeros, ones, zeros], axis=1),
                       jnp.concatenate([zeros, v, zeros, ones], axis=1)))
        c0 = kv * ATT_GROUP * HEAD_DIM
        qp = jnp.concatenate([q_ref[:, c0 + p * slab:c0 + (p + 1) * slab] for p in range(pairs)],
                             axis=0)
        s_ref[2 * kv] = _dot_nt(qp, jnp.concatenate([k, zeros], axis=1))
        s_ref[2 * kv + 1] = _dot_nt(qp, jnp.concatenate([zeros, k], axis=1))

    for kv in range(n_kv):
        c0 = kv * ATT_GROUP * HEAD_DIM

        def probs(parity):
            sink = jnp.concatenate(
                [jnp.full((1, 1, 1), sink_ref[kv * ATT_GROUP + 2 * p + parity], F32)
                 for p in range(pairs)], axis=0)
            fill = jnp.where(sink_col[None], sink, -jnp.inf)
            s = jnp.where(valid, s_ref[2 * kv + parity].reshape(pairs, W, 2 * W), fill)
            m = jnp.max(s, axis=-1, keepdims=True)
            return jnp.exp(s - m).astype(BF16).reshape(pairs * W, 2 * W)

        nd = _dot(probs(0), v_wide[kv][0]) + _dot(probs(1), v_wide[kv][1])
        out = (nd[:, :slab] / nd[:, slab:]).astype(BF16)
        for p in range(pairs):
            o_ref[:, c0 + p * slab:c0 + (p + 1) * slab] = out[p * W:(p + 1) * W]


def _attention(q, k, v, sinks, side_a, side_b, *, batch):
    t, dq = q.shape
    dk = k.shape[1]
    nb = t // batch // WINDOW
    own = lambda b, n: (b * nb + n, 0)
    prev = lambda b, n: (b * nb + jnp.maximum(n - 1, 0), 0)
    step = lambda b, n: b * nb + n
    spec_a, shape_a = _ride_along_cast(side_a, batch * nb, step)
    spec_b, shape_b = _ride_along_cast(side_b, batch * nb, step)
    return pl.pallas_call(
        functools.partial(_attn_kernel, n_kv=dk // HEAD_DIM),
        out_shape=(jax.ShapeDtypeStruct((t, dq), BF16), shape_a, shape_b),
        grid=(batch, nb),
        in_specs=[pl.BlockSpec(memory_space=pltpu.SMEM),
                  pl.BlockSpec((WINDOW, dq), own),
                  pl.BlockSpec((WINDOW, dk), prev),
                  pl.BlockSpec((WINDOW, dk), own),
                  pl.BlockSpec((WINDOW, dk), prev),
                  pl.BlockSpec((WINDOW, dk), own),
                  spec_a, spec_b],
        out_specs=(pl.BlockSpec((WINDOW, dq), own), spec_a, spec_b),
        scratch_shapes=[pltpu.VMEM((2 * (dk // HEAD_DIM), ATT_GROUP // 2 * WINDOW, 2 * WINDOW), F32)],
        compiler_params=_params("arbitrary", "arbitrary"),
        name="swa_attention",
    )(sinks.astype(F32), q, k, k, v, v, side_a, side_b)


def _proj_residual_kernel(x_ref, a_ref, w_ref, o_ref):
    o_ref[...] = x_ref[...] + _dot(a_ref[...], w_ref[...])


def _proj_residual(x, a, w, *, bm, bn):
    t, d = x.shape
    k = a.shape[1]
    return pl.pallas_call(
        _proj_residual_kernel,
        out_shape=jax.ShapeDtypeStruct((t, d), F32),
        grid=(d // bn, t // bm),
        in_specs=[pl.BlockSpec((bm, bn), lambda j, i: (i, j)),
                  pl.BlockSpec((bm, k), lambda j, i: (i, 0)),
                  pl.BlockSpec((k, bn), lambda j, i: (0, j))],
        out_specs=pl.BlockSpec((bm, bn), lambda j, i: (i, j)),
        compiler_params=_params("arbitrary", "arbitrary"),
        name="proj_residual",
    )(x, a, w)


def _ffn_kernel(x_ref, g_ref, wg_ref, wu_ref, wd_ref, o_ref, h_ref):
    f = pl.program_id(1)

    @pl.when(f == 0)
    def _():
        x = x_ref[...]
        h_ref[...] = _rms_norm_bf16(x, g_ref[...])
        o_ref[...] = x

    h = h_ref[...]
    gate = _dot(h, wg_ref[...])
    up = _dot(h, wu_ref[...])
    act = (gate * jax.nn.sigmoid(gate) * up).astype(BF16)
    o_ref[...] += _dot(act, wd_ref[...])


def _ffn(x, gain, w_gate, w_up, w_down, *, layer, bm, bf):
    t, d = x.shape
    dff = w_gate.shape[2]
    return pl.pallas_call(
        _ffn_kernel,
        out_shape=jax.ShapeDtypeStruct((t, d), F32),
        grid=(t // bm, dff // bf),
        in_specs=[pl.BlockSpec((bm, d), lambda i, f: (i, 0)),
                  pl.BlockSpec((1, d), lambda i, f: (0, 0)),
                  pl.BlockSpec((None, d, bf), lambda i, f: (layer, 0, f)),
                  pl.BlockSpec((None, d, bf), lambda i, f: (layer, 0, f)),
                  pl.BlockSpec((None, bf, d), lambda i, f: (layer, f, 0))],
        out_specs=pl.BlockSpec((bm, d), lambda i, f: (i, 0)),
        scratch_shapes=[pltpu.VMEM((bm, d), BF16)],
        compiler_params=_params("arbitrary", "arbitrary"),
        name="swiglu_ffn",
    )(x, gain[None, :], w_gate, w_up, w_down)


def _softplus(x):
    return jnp.maximum(x, 0.0) + jnp.log1p(jnp.exp(-jnp.abs(x)))


def _silu(x):
    return x * jax.nn.sigmoid(x)


def _bf16_terms(x, n):
    terms = []
    for _ in range(n):
        t = x.astype(BF16).astype(F32)
        terms.append(t)
        x = x - t
    return terms


def _ssm_gate_kernel(x_ref, g_ref, w_ref, wdtt_ref, dtb_ref, alog_row_ref, alog_col_ref,
                     zs_ref, h_ref, acum_t_ref, acum3_ref, dt2_ref, *, chunk):
    @pl.when(pl.program_id(1) == 0)
    def _():
        h = _rms_norm_bf16(x_ref[...], g_ref[...])
        h_ref[...] = h
        heads = wdtt_ref.shape[0]
        L = chunk
        dt_t = _softplus(_dot_nt(wdtt_ref[...], h) + dtb_ref[...])
        dt_c = dt_t.T
        dta_c = dt_c * (-LOG2_E * jnp.exp(alog_row_ref[...]))
        dta_t = dt_t * (-LOG2_E * jnp.exp(alog_col_ref[...]))
        r_i = lax.broadcasted_iota(jnp.int32, (L, L), 0)
        c_i = lax.broadcasted_iota(jnp.int32, (L, L), 1)
        upper = (r_i <= c_i).astype(BF16)
        lower = (r_i >= c_i).astype(BF16)
        for r0 in range(0, x_ref.shape[0], L):
            rows = slice(r0, r0 + L)
            stacked = jnp.concatenate(_bf16_terms(dta_t[:, rows], 3), axis=0).astype(BF16)
            sums = _dot(stacked, upper)
            acum_t_ref[:, rows] = sums[:heads] + sums[heads:2 * heads] + sums[2 * heads:]
            acum_c = sum(_dot(lower, term.astype(BF16)) for term in _bf16_terms(dta_c[rows], 3))
            acum3_ref[rows, :] = jnp.concatenate(_bf16_terms(acum_c, 3), axis=1).astype(BF16)
        dt2_ref[...] = jnp.concatenate(_bf16_terms(dt_c, 2), axis=1).astype(BF16)

    zs_ref[...] = _silu(_dot(h_ref[...], w_ref[...])).astype(BF16)


def _ssm_gate_proj(x, gain, w_in, w_dt, dt_bias, a_log, *, bm, bn, n):
    t, d = x.shape
    heads = w_dt.shape[1]
    fixed = lambda i, j: (0, 0)
    row_tile = lambda i, j: (i, 0)
    f32 = lambda v: v.astype(F32)
    return pl.pallas_call(
        functools.partial(_ssm_gate_kernel, chunk=SSM_CHUNK),
        out_shape=(jax.ShapeDtypeStruct((t, n), BF16),
                   jax.ShapeDtypeStruct((t, d), BF16),
                   jax.ShapeDtypeStruct((heads, t), F32),
                   jax.ShapeDtypeStruct((t, 3 * heads), BF16),
                   jax.ShapeDtypeStruct((t, 2 * heads), BF16)),
        grid=(t // bm, n // bn),
        in_specs=[pl.BlockSpec((bm, d), row_tile),
                  pl.BlockSpec((1, d), fixed),
                  pl.BlockSpec((d, bn), lambda i, j: (0, j)),
                  pl.BlockSpec((heads, d), fixed),
                  pl.BlockSpec((heads, 1), fixed),
                  pl.BlockSpec((1, heads), fixed),
                  pl.BlockSpec((heads, 1), fixed)],
        out_specs=(pl.BlockSpec((bm, bn), lambda i, j: (i, j)),
                   pl.BlockSpec((bm, d), row_tile),
                   pl.BlockSpec((heads, bm), lambda i, j: (0, i)),
                   pl.BlockSpec((bm, 3 * heads), row_tile),
                   pl.BlockSpec((bm, 2 * heads), row_tile)),
        compiler_params=_params("arbitrary", "arbitrary"),
        name="ssm_gate_proj",
    )(x, gain[None, :], w_in, w_dt.T, f32(dt_bias)[:, None], f32(a_log)[None, :], f32(a_log)[:, None])


def _ssm_conv_kernel(h_ref, w_ref, cw_ref, cb_ref, o_ref, acc_ref, carry_ref, *, tiles_per_seq):
    i = pl.program_id(0)
    j = pl.program_id(1)
    bm = h_ref.shape[0]
    acc_ref[...] = _dot(h_ref[...], w_ref[...])
    acc = acc_ref[...]
    row8 = lax.broadcasted_iota(jnp.int32, (8, 1), 0)
    tail = jnp.where(i % tiles_per_seq != 0, carry_ref[j], 0.0)
    carry_ref[j] = acc[bm - 8:, :]
    def shift_rows(v, v_tail, back):
        rolled = pltpu.roll(v, back, 0)
        head = jnp.where(row8 < back, pltpu.roll(v_tail, back, 0), rolled[:8])
        return jnp.concatenate([head, rolled[8:]], axis=0)

    w0, w1, w2, w3 = (cw_ref[k:k + 1, :] for k in range(SSM_CONV))
    a1 = shift_rows(acc, tail, 1)
    q = w1 * acc + w0 * a1
    q_tail = w1 * tail + w0 * pltpu.roll(tail, 1, 0)
    out = cb_ref[...] + w3 * acc + w2 * a1 + shift_rows(q, q_tail, 2)
    o_ref[...] = _silu(out).astype(BF16)


def _ssm_conv_proj(h, w_in, conv_w, conv_b, *, bm, bn, seq, col0, n):
    t, d = h.shape
    return pl.pallas_call(
        functools.partial(_ssm_conv_kernel, tiles_per_seq=seq // bm),
        out_shape=jax.ShapeDtypeStruct((t, n), BF16),
        grid=(t // bm, n // bn),
        in_specs=[pl.BlockSpec((bm, d), lambda i, j: (i, 0)),
                  pl.BlockSpec((d, bn), lambda i, j: (0, col0 + j)),
                  pl.BlockSpec((SSM_CONV, bn), lambda i, j: (0, j)),
                  pl.BlockSpec((1, bn), lambda i, j: (0, j))],
        out_specs=pl.BlockSpec((bm, bn), lambda i, j: (i, j)),
        scratch_shapes=[pltpu.VMEM((bm, bn), F32), pltpu.VMEM((n // bn, 8, bn), F32)],
        compiler_params=_params("arbitrary", "arbitrary"),
        name="ssm_conv_proj",
    )(h, w_in, conv_w.astype(F32), conv_b.astype(F32)[None, :])


def _ssd_kernel(zs_ref, xs_ref, b_ref, c_ref, acum_t_ref, acum3_ref, dt2_ref, ex3_ref, ex2_ref,
                d_ref, ng_ref, y_ref, state_ref, *, chunk, rpg, gps):
    c = pl.program_id(1)
    L, H, N = chunk, chunk // 2, SSM_STATE
    gw = rpg * HEAD_DIM
    tri = (lax.broadcasted_iota(jnp.int32, (H, H), 0) >= lax.broadcasted_iota(jnp.int32, (H, H), 1))
    acum3 = acum3_ref[...]
    dt2 = dt2_ref[...]
    g0 = pl.program_id(2) * gps

    @pl.when(c == 0)
    def _():
        for u in range(gps):
            state_ref[g0 + u] = jnp.zeros(state_ref.shape[1:], F32)

    for u in range(gps):
        g = g0 + u
        lanes = slice(u * gw, (u + 1) * gw)
        xs = xs_ref[:, lanes].astype(F32)
        bm = b_ref[:, u * N:(u + 1) * N]
        cm = c_ref[:, u * N:(u + 1) * N]

        acum_t = acum_t_ref[u * rpg:(u + 1) * rpg, :]
        acum = _dot(acum3, ex3_ref[u])
        dt_full = _dot(dt2, ex2_ref[u])
        acum_last = acum[L - 1:L, :]

        xdt = xs * dt_full
        xdt_b = xdt.astype(BF16)
        cb = _dot_nt(cm, bm)
        y_parts = []
        for r in range(rpg):
            col = acum[:, r * HEAD_DIM:r * HEAD_DIM + 1]
            row = acum_t[r:r + 1, :]
            d00 = jnp.exp2(jnp.where(tri, col[:H] - row[:, :H], -jnp.inf))
            d11 = jnp.exp2(jnp.where(tri, col[H:] - row[:, H:], -jnp.inf))
            d10 = jnp.exp2(col[H:] - row[:, :H])
            m00 = (cb[:H, :H] * d00).astype(BF16)
            m1 = jnp.concatenate([cb[H:, :H] * d10, cb[H:, H:] * d11], axis=1).astype(BF16)
            x_r = xdt_b[:, r * HEAD_DIM:(r + 1) * HEAD_DIM]
            y_parts.append(jnp.concatenate([_dot(m00, x_r[:H]), _dot(m1, x_r)], axis=0))
        y = jnp.concatenate(y_parts, axis=1)

        state = state_ref[g]
        y = y + _dot(cm, state.astype(BF16)) * jnp.exp2(acum)
        to_end = jnp.exp2(acum_last - acum)
        state_ref[g] = state * jnp.exp2(acum_last) + _dot_tn(bm, (xdt * to_end).astype(BF16))

        y = (y + d_ref[:, lanes] * xs) * zs_ref[:, lanes].astype(F32)
        y = y * lax.rsqrt(jnp.mean(y * y, axis=-1, keepdims=True) + EPS)
        y_ref[:, lanes] = (y * ng_ref[:, lanes]).astype(BF16)


def _ssd(zs, xbc, acum_t, acum3, dt2, d_skip, norm_g, *, batch, gps):
    t, d_inner = zs.shape
    heads = acum_t.shape[0]
    G, N = SSM_GROUPS, SSM_STATE
    rpg = heads // G
    gw = d_inner // G
    L = SSM_CHUNK
    nc = t // batch // L
    b0 = d_inner // (gps * N)
    c0 = b0 + G // gps

    def cur(off):
        return lambda b, c, g: (b * nc + c, off + g)

    per_group = lambda b, c, g: (0, g)
    fixed_rows = lambda b, c, g: (b * nc + c, 0)
    lane_head = jnp.arange(G)[:, None, None] * rpg + jnp.arange(gw)[None, None, :] // HEAD_DIM
    one_hot = (lane_head == jnp.arange(heads)[None, :, None]).astype(BF16)
    rep = lambda v: jnp.repeat(v.astype(F32), HEAD_DIM)[None, :]
    return pl.pallas_call(
        functools.partial(_ssd_kernel, chunk=L, rpg=rpg, gps=gps),
        out_shape=jax.ShapeDtypeStruct((t, d_inner), BF16),
        grid=(batch, nc, G // gps),
        in_specs=[pl.BlockSpec((L, gps * gw), cur(0)),
                  pl.BlockSpec((L, gps * gw), cur(0)),
                  pl.BlockSpec((L, gps * N), cur(b0)),
                  pl.BlockSpec((L, gps * N), cur(c0)),
                  pl.BlockSpec((gps * rpg, L), lambda b, c, g: (g, b * nc + c)),
                  pl.BlockSpec((L, 3 * heads), fixed_rows),
                  pl.BlockSpec((L, 2 * heads), fixed_rows),
                  pl.BlockSpec((gps, 3 * heads, gw), lambda b, c, g: (g, 0, 0)),
                  pl.BlockSpec((gps, 2 * heads, gw), lambda b, c, g: (g, 0, 0)),
                  pl.BlockSpec((1, gps * gw), per_group),
                  pl.BlockSpec((1, gps * gw), per_group)],
        out_specs=pl.BlockSpec((L, gps * gw), cur(0)),
        scratch_shapes=[pltpu.VMEM((G, N, gw), F32)],
        compiler_params=_params("arbitrary", "arbitrary", "arbitrary"),
        name="ssd_scan",
    )(zs, xbc, xbc, xbc, acum_t, acum3, dt2, jnp.tile(one_hot, (1, 3, 1)), jnp.tile(one_hot, (1, 2, 1)),
      rep(d_skip), norm_g.astype(F32)[None, :])


def kernel(x, positions, mixer_norm, ffn_norm, attn_w_qkv, attn_q_norm, attn_k_norm, attn_sinks,
           attn_w_o, ssm_w_in, ssm_conv_w, ssm_conv_b, ssm_dt_bias, ssm_a_log, ssm_d, ssm_norm,
           ssm_w_out, ffn_w_gate, ffn_w_up, ffn_w_down):
    batch, seq, d = x.shape
    t = batch * seq
    xf = x.reshape(t, d)
    pos = positions.reshape(t, 1)
    d_inner = ssm_w_out.shape[1]
    main_w = 2 * d_inner + 2 * SSM_GROUPS * SSM_STATE
    bm = _pick(t, (512, 256, 128))
    bm_big = _pick(seq, (1024, 512, 256, 128))
    bf = _pick(ffn_w_gate.shape[2], (512, 256))
    bn = _pick(d, (1024, 512))

    def ffn(xf, i):
        return _ffn(xf, ffn_norm[i], w_gate, w_up, w_down, layer=i, bm=bm_big, bf=bf)

    q, k, v, w_down = _qkv_proj(xf, pos, mixer_norm[0], attn_w_qkv[0].astype(BF16),
                                attn_q_norm[0], attn_k_norm[0], ffn_w_down, bm=bm)
    att, w_gate, w_up = _attention(q, k, v, attn_sinks[0], ffn_w_gate, ffn_w_up, batch=batch)
    xf = _proj_residual(xf, att, attn_w_o[0].astype(BF16), bm=bm_big, bn=bn)
    xf = ffn(xf, 0)

    w_in = ssm_w_in[0].astype(BF16)
    bn_in = _pick(SSM_GROUPS * SSM_STATE, (1024, 512, 256))
    zs, h, acum_t, acum3, dt2 = _ssm_gate_proj(
        xf, mixer_norm[1], w_in, w_in[:, main_w:], ssm_dt_bias[0], ssm_a_log[0],
        bm=bm_big, bn=bn_in, n=d_inner)
    xbc = _ssm_conv_proj(h, w_in, ssm_conv_w[0], ssm_conv_b[0],
                         bm=bm_big, bn=bn_in, seq=seq, col0=d_inner // bn_in, n=main_w - d_inner)
    y = _ssd(zs, xbc, acum_t, acum3, dt2, ssm_d[0], ssm_norm[0], batch=batch, gps=4)
    xf = _proj_residual(xf, y, ssm_w_out[0].astype(BF16), bm=bm, bn=bn)
    xf = ffn(xf, 1)
    return xf.reshape(batch, seq, d)
```

```python
import functools

import jax
import jax.numpy as jnp
from jax import lax
from jax.experimental import pallas as pl
from jax.experimental.pallas import tpu as pltpu

EPS = 1e-6
HEAD_DIM = 64
ATT_GROUP = 8
WINDOW = 128
ROPE_THETA = 10000.0
SSM_GROUPS = 8
SSM_STATE = 128
SSM_CONV = 4
SSM_CHUNK = 256
LOG2_E = 1.4426950408889634
LANES = 128
VMEM_LIMIT_BYTES = 56 * 1024 * 1024

F32 = jnp.float32
BF16 = jnp.bfloat16


def _params(*sem):
    return pltpu.CompilerParams(dimension_semantics=sem, vmem_limit_bytes=VMEM_LIMIT_BYTES)


def _pick(n, candidates):
    for c in candidates:
        if n % c == 0:
            return c
    raise ValueError(f"no tile in {candidates} divides {n}")


def _ride_along_cast(w, steps, step_index):
    layers, rows, cols = w.shape
    slab = layers * rows // steps
    per_layer = rows // slab
    assert slab * steps == layers * rows and per_layer * slab == rows and slab % 16 == 0
    imap = lambda *g: (step_index(*g) // per_layer, step_index(*g) % per_layer, 0)
    spec = pl.BlockSpec((None, slab, cols), imap)
    return spec, jax.ShapeDtypeStruct(w.shape, BF16)


def _rms_norm_bf16(x, gain):
    y = x * lax.rsqrt(jnp.mean(x * x, axis=-1, keepdims=True) + EPS)
    return (y * gain).astype(BF16)


def _dot(a, b):
    return jnp.dot(a, b, preferred_element_type=F32)


def _dot_nt(a, b):
    return lax.dot_general(a, b, (((1,), (1,)), ((), ())), preferred_element_type=F32)


def _dot_tn(a, b):
    return lax.dot_general(a, b, (((0,), (0,)), ((), ())), preferred_element_type=F32)


def _qkv_kernel(x_ref, pos_ref, g_ref, w_ref, invf_ref, qg_ref, kg_ref, side_ref,
                q_ref, k_ref, v_ref, side16_ref, acc_ref, *, dq, dk, cw):
    side16_ref[...] = side_ref[...].astype(BF16)
    h = _rms_norm_bf16(x_ref[...], g_ref[...])
    ang = pos_ref[...].astype(F32) * invf_ref[...]
    lane = lax.broadcasted_iota(jnp.int32, (1, LANES), 1)
    first_half = (lane % HEAD_DIM) < (HEAD_DIM // 2)
    cos = jnp.cos(ang)
    sin = jnp.sin(ang)
    sin_signed = jnp.where(first_half, -sin, sin)

    r_i = lax.broadcasted_iota(jnp.int32, (cw, cw), 0)
    c_i = lax.broadcasted_iota(jnp.int32, (cw, cw), 1)
    same_head = (r_i // HEAD_DIM == c_i // HEAD_DIM).astype(BF16)

    def norm_rope(a, gain, scale):
        sq = a * a
        sq_hi = sq.astype(BF16)
        sq_lo = (sq - sq_hi.astype(F32)).astype(BF16)
        ss = _dot(sq_hi, same_head) + _dot(sq_lo, same_head)
        outs = []
        for s0 in range(0, cw, LANES):
            sl = slice(s0, s0 + LANES)
            y = a[:, sl] * lax.rsqrt(ss[:, sl] * (1.0 / HEAD_DIM) + EPS) * gain
            partner = jnp.where(first_half,
                                pltpu.roll(y, LANES - HEAD_DIM // 2, 1),
                                pltpu.roll(y, HEAD_DIM // 2, 1))
            outs.append(((y * cos + partner * sin_signed) * scale).astype(BF16))
        return jnp.concatenate(outs, axis=1)

    q_scale = HEAD_DIM ** -0.5
    acc_ref[...] = _dot(h, w_ref[:, :dq + dk])
    v_ref[...] = _dot(h, w_ref[:, dq + dk:]).astype(BF16)
    for c0 in range(0, dq + dk, cw):
        acc = acc_ref[:, c0:c0 + cw]
        if c0 < dq:
            q_ref[:, c0:c0 + cw] = norm_rope(acc, qg_ref[...], q_scale)
        else:
            k_ref[:, c0 - dq:c0 - dq + cw] = norm_rope(acc, kg_ref[...], 1.0)


def _qkv_proj(x, pos, gain, w, q_gain, k_gain, side, *, bm):
    t, d = x.shape
    dq = d
    dk = d // ATT_GROUP
    cw = _pick(dk, (256, 128))
    inv_freq = ROPE_THETA ** (-jnp.arange(0, HEAD_DIM, 2, dtype=F32) / HEAD_DIM)
    invf = jnp.tile(inv_freq, LANES // (HEAD_DIM // 2))[None, :]
    tile_gain = lambda g: jnp.tile(g.astype(F32), LANES // HEAD_DIM)[None, :]
    row = lambda i: (i, 0)
    fixed = lambda i: (0, 0)
    side_spec, side_shape = _ride_along_cast(side, t // bm, lambda i: i)
    return pl.pallas_call(
        functools.partial(_qkv_kernel, dq=dq, dk=dk, cw=cw),
        out_shape=(jax.ShapeDtypeStruct((t, dq), BF16),
                   jax.ShapeDtypeStruct((t, dk), BF16),
                   jax.ShapeDtypeStruct((t, dk), BF16),
                   side_shape),
        grid=(t // bm,),
        in_specs=[pl.BlockSpec((bm, d), row),
                  pl.BlockSpec((bm, 1), row),
                  pl.BlockSpec((1, d), fixed),
                  pl.BlockSpec((d, dq + 2 * dk), fixed),
                  pl.BlockSpec((1, LANES), fixed),
                  pl.BlockSpec((1, LANES), fixed),
                  pl.BlockSpec((1, LANES), fixed),
                  side_spec],
        out_specs=(pl.BlockSpec((bm, dq), row),
                   pl.BlockSpec((bm, dk), row),
                   pl.BlockSpec((bm, dk), row),
                   side_spec),
        scratch_shapes=[pltpu.VMEM((bm, dq + dk), F32)],
        compiler_params=_params("arbitrary"),
        name="qkv_proj",
    )(x, pos, gain[None, :], w, invf, tile_gain(q_gain), tile_gain(k_gain), side)


def _attn_kernel(sink_ref, q_ref, kp_ref, ko_ref, vp_ref, vo_ref, side_a_ref, side_b_ref,
                 o_ref, side_a16_ref, side_b16_ref, s_ref, *, n_kv):
    side_a16_ref[...] = side_a_ref[...].astype(BF16)
    side_b16_ref[...] = side_b_ref[...].astype(BF16)
    n = pl.program_id(1)
    W, pairs, slab = WINDOW, ATT_GROUP // 2, 2 * HEAD_DIM
    qi = lax.broadcasted_iota(jnp.int32, (W, 2 * W), 0)
    kj = lax.broadcasted_iota(jnp.int32, (W, 2 * W), 1)
    lag = qi + W - kj
    first_key = jnp.where(n > 0, 0, W)
    valid = ((lag >= 0) & (lag < W) & (kj >= first_key))[None]
    sink_col = kj[:1] == 0
    key_row = lax.broadcasted_iota(jnp.int32, (2 * W, 1), 0)
    zeros = jnp.zeros((2 * W, HEAD_DIM), BF16)
    ones = jnp.ones((2 * W, HEAD_DIM), BF16)
    v_wide = []
    for kv in range(n_kv):
        ks = slice(kv * HEAD_DIM, (kv + 1) * HEAD_DIM)
        k = jnp.concatenate([kp_ref[:, ks], ko_ref[:, ks]], axis=0)
        v = jnp.concatenate([vp_ref[:, ks], vo_ref[:, ks]], axis=0)
        v = jnp.where(key_row == 0, jnp.zeros_like(v), v)
        v_wide.append((jnp.concatenate([v, zeros, ones, zeros], axis=1),
                       jnp.concatenate([zeros, v, zeros, ones], axis=1)))
        c0 = kv * ATT_GROUP * HEAD_DIM
        qp = jnp.concatenate([q_ref[:, c0 + p * slab:c0 + (p + 1) * slab] for p in range(pairs)],
                             axis=0)
        s_ref[2 * kv] = _dot_nt(qp, jnp.concatenate([k, zeros], axis=1))
        s_ref[2 * kv + 1] = _dot_nt(qp, jnp.concatenate([zeros, k], axis=1))

    for kv in range(n_kv):
        c0 = kv * ATT_GROUP * HEAD_DIM

        def probs(parity):
            sink = jnp.concatenate(
                [jnp.full((1, 1, 1), sink_ref[kv * ATT_GROUP + 2 * p + parity], F32)
                 for p in range(pairs)], axis=0)
            fill = jnp.where(sink_col[None], sink, -jnp.inf)
            s = jnp.where(valid, s_ref[2 * kv + parity].reshape(pairs, W, 2 * W), fill)
            m = jnp.max(s, axis=-1, keepdims=True)
            return jnp.exp(s - m).astype(BF16).reshape(pairs * W, 2 * W)

        nd = _dot(probs(0), v_wide[kv][0]) + _dot(probs(1), v_wide[kv][1])
        out = (nd[:, :slab] / nd[:, slab:]).astype(BF16)
        for p in range(pairs):
            o_ref[:, c0 + p * slab:c0 + (p + 1) * slab] = out[p * W:(p + 1) * W]


def _attention(q, k, v, sinks, side_a, side_b, *, batch):
    t, dq = q.shape
    dk = k.shape[1]
    nb = t // batch // WINDOW
    own = lambda b, n: (b * nb + n, 0)
    prev = lambda b, n: (b * nb + jnp.maximum(n - 1, 0), 0)
    step = lambda b, n: b * nb + n
    spec_a, shape_a = _ride_along_cast(side_a, batch * nb, step)
    spec_b, shape_b = _ride_along_cast(side_b, batch * nb, step)
    return pl.pallas_call(
        functools.partial(_attn_kernel, n_kv=dk // HEAD_DIM),
        out_shape=(jax.ShapeDtypeStruct((t, dq), BF16), shape_a, shape_b),
        grid=(batch, nb),
        in_specs=[pl.BlockSpec(memory_space=pltpu.SMEM),
                  pl.BlockSpec((WINDOW, dq), own),
                  pl.BlockSpec((WINDOW, dk), prev),
                  pl.BlockSpec((WINDOW, dk), own),
                  pl.BlockSpec((WINDOW, dk), prev),
                  pl.BlockSpec((WINDOW, dk), own),
                  spec_a, spec_b],
        out_specs=(pl.BlockSpec((WINDOW, dq), own), spec_a, spec_b),
        scratch_shapes=[pltpu.VMEM((2 * (dk // HEAD_DIM), ATT_GROUP // 2 * WINDOW, 2 * WINDOW), F32)],
        compiler_params=_params("arbitrary", "arbitrary"),
        name="swa_attention",
    )(sinks.astype(F32), q, k, k, v, v, side_a, side_b)


def _proj_residual_kernel(x_ref, a_ref, w_ref, o_ref):
    o_ref[...] = x_ref[...] + _dot(a_ref[...], w_ref[...])


def _proj_residual(x, a, w, *, bm, bn):
    t, d = x.shape
    k = a.shape[1]
    return pl.pallas_call(
        _proj_residual_kernel,
        out_shape=jax.ShapeDtypeStruct((t, d), F32),
        grid=(d // bn, t // bm),
        in_specs=[pl.BlockSpec((bm, bn), lambda j, i: (i, j)),
                  pl.BlockSpec((bm, k), lambda j, i: (i, 0)),
                  pl.BlockSpec((k, bn), lambda j, i: (0, j))],
        out_specs=pl.BlockSpec((bm, bn), lambda j, i: (i, j)),
        compiler_params=_params("arbitrary", "arbitrary"),
        name="proj_residual",
    )(x, a, w)


def _ffn_kernel(x_ref, g_ref, wg_ref, wu_ref, wd_ref, o_ref, h_ref):
    f = pl.program_id(1)

    @pl.when(f == 0)
    def _():
        x = x_ref[...]
        h_ref[...] = _rms_norm_bf16(x, g_ref[...])
        o_ref[...] = x

    h = h_ref[...]
    gate = _dot(h, wg_ref[...])
    up = _dot(h, wu_ref[...])
    act = (gate * jax.nn.sigmoid(gate) * up).astype(BF16)
    o_ref[...] += _dot(act, wd_ref[...])


def _ffn(x, gain, w_gate, w_up, w_down, *, layer, bm, bf):
    t, d = x.shape
    dff = w_gate.shape[2]
    return pl.pallas_call(
        _ffn_kernel,
        out_shape=jax.ShapeDtypeStruct((t, d), F32),
        grid=(t // bm, dff // bf),
        in_specs=[pl.BlockSpec((bm, d), lambda i, f: (i, 0)),
                  pl.BlockSpec((1, d), lambda i, f: (0, 0)),
                  pl.BlockSpec((None, d, bf), lambda i, f: (layer, 0, f)),
                  pl.BlockSpec((None, d, bf), lambda i, f: (layer, 0, f)),
                  pl.BlockSpec((None, bf, d), lambda i, f: (layer, f, 0))],
        out_specs=pl.BlockSpec((bm, d), lambda i, f: (i, 0)),
        scratch_shapes=[pltpu.VMEM((bm, d), BF16)],
        compiler_params=_params("arbitrary", "arbitrary"),
        name="swiglu_ffn",
    )(x, gain[None, :], w_gate, w_up, w_down)


def _softplus(x):
    return jnp.maximum(x, 0.0) + jnp.log1p(jnp.exp(-jnp.abs(x)))


def _silu(x):
    return x * jax.nn.sigmoid(x)


def _bf16_terms(x, n):
    terms = []
    for _ in range(n):
        t = x.astype(BF16).astype(F32)
        terms.append(t)
        x = x - t
    return terms


def _ssm_gate_kernel(x_ref, g_ref, w_ref, wdtt_ref, dtb_ref, alog_row_ref, alog_col_ref,
                     zs_ref, h_ref, acum_t_ref, acum_c_ref, dt_c_ref, *, chunk):
    @pl.when(pl.program_id(1) == 0)
    def _():
        h = _rms_norm_bf16(x_ref[...], g_ref[...])
        h_ref[...] = h
        heads = wdtt_ref.shape[0]
        L = chunk
        dt_t = _softplus(_dot_nt(wdtt_ref[...], h) + dtb_ref[...])
        dt_c = dt_t.T
        dta_c = dt_c * (-LOG2_E * jnp.exp(alog_row_ref[...]))
        dta_t = dt_t * (-LOG2_E * jnp.exp(alog_col_ref[...]))
        r_i = lax.broadcasted_iota(jnp.int32, (L, L), 0)
        c_i = lax.broadcasted_iota(jnp.int32, (L, L), 1)
        upper = (r_i <= c_i).astype(BF16)
        lower = (r_i >= c_i).astype(BF16)
        for r0 in range(0, x_ref.shape[0], L):
            rows = slice(r0, r0 + L)
            stacked = jnp.concatenate(_bf16_terms(dta_t[:, rows], 3), axis=0).astype(BF16)
            sums = _dot(stacked, upper)
            acum_t_ref[:, rows] = sums[:heads] + sums[heads:2 * heads] + sums[2 * heads:]
            acum_c = sum(_dot(lower, term.astype(BF16)) for term in _bf16_terms(dta_c[rows], 3))
            acum_c_ref[rows, :] = acum_c
        dt_c_ref[...] = dt_c

    zs_ref[...] = _silu(_dot(h_ref[...], w_ref[...])).astype(BF16)


def _ssm_gate_proj(x, gain, w_in, w_dt, dt_bias, a_log, *, bm, bn, n):
    t, d = x.shape
    heads = w_dt.shape[1]
    fixed = lambda i, j: (0, 0)
    row_tile = lambda i, j: (i, 0)
    f32 = lambda v: v.astype(F32)
    return pl.pallas_call(
        functools.partial(_ssm_gate_kernel, chunk=SSM_CHUNK),
        out_shape=(jax.ShapeDtypeStruct((t, n), BF16),
                   jax.ShapeDtypeStruct((t, d), BF16),
                   jax.ShapeDtypeStruct((heads, t), F32),
                   jax.ShapeDtypeStruct((t, heads), F32),
                   jax.ShapeDtypeStruct((t, heads), F32)),
        grid=(t // bm, n // bn),
        in_specs=[pl.BlockSpec((bm, d), row_tile),
                  pl.BlockSpec((1, d), fixed),
                  pl.BlockSpec((d, bn), lambda i, j: (0, j)),
                  pl.BlockSpec((heads, d), fixed),
                  pl.BlockSpec((heads, 1), fixed),
                  pl.BlockSpec((1, heads), fixed),
                  pl.BlockSpec((heads, 1), fixed)],
        out_specs=(pl.BlockSpec((bm, bn), lambda i, j: (i, j)),
                   pl.BlockSpec((bm, d), row_tile),
                   pl.BlockSpec((heads, bm), lambda i, j: (0, i)),
                   pl.BlockSpec((bm, heads), row_tile),
                   pl.BlockSpec((bm, heads), row_tile)),
        compiler_params=_params("arbitrary", "arbitrary"),
        name="ssm_gate_proj",
    )(x, gain[None, :], w_in, w_dt.T, f32(dt_bias)[:, None], f32(a_log)[None, :], f32(a_log)[:, None])


def _ssm_conv_kernel(h_ref, w_ref, cw_ref, cb_ref, o_ref, acc_ref, carry_ref, *, tiles_per_seq):
    i = pl.program_id(0)
    j = pl.program_id(1)
    bm = h_ref.shape[0]
    acc_ref[...] = _dot(h_ref[...], w_ref[...])
    acc = acc_ref[...]
    row8 = lax.broadcasted_iota(jnp.int32, (8, 1), 0)
    tail = jnp.where(i % tiles_per_seq != 0, carry_ref[j], 0.0)
    carry_ref[j] = acc[bm - 8:, :]
    def shift_rows(v, v_tail, back):
        rolled = pltpu.roll(v, back, 0)
        head = jnp.where(row8 < back, pltpu.roll(v_tail, back, 0), rolled[:8])
        return jnp.concatenate([head, rolled[8:]], axis=0)

    w0, w1, w2, w3 = (cw_ref[k:k + 1, :] for k in range(SSM_CONV))
    a1 = shift_rows(acc, tail, 1)
    q = w1 * acc + w0 * a1
    q_tail = w1 * tail + w0 * pltpu.roll(tail, 1, 0)
    out = cb_ref[...] + w3 * acc + w2 * a1 + shift_rows(q, q_tail, 2)
    o_ref[...] = _silu(out).astype(BF16)


def _ssm_conv_proj(h, w_in, conv_w, conv_b, *, bm, bn, seq, col0, n):
    t, d = h.shape
    return pl.pallas_call(
        functools.partial(_ssm_conv_kernel, tiles_per_seq=seq // bm),
        out_shape=jax.ShapeDtypeStruct((t, n), BF16),
        grid=(t // bm, n // bn),
        in_specs=[pl.BlockSpec((bm, d), lambda i, j: (i, 0)),
                  pl.BlockSpec((d, bn), lambda i, j: (0, col0 + j)),
                  pl.BlockSpec((SSM_CONV, bn), lambda i, j: (0, j)),
                  pl.BlockSpec((1, bn), lambda i, j: (0, j))],
        out_specs=pl.BlockSpec((bm, bn), lambda i, j: (i, j)),
        scratch_shapes=[pltpu.VMEM((bm, bn), F32), pltpu.VMEM((n // bn, 8, bn), F32)],
        compiler_params=_params("arbitrary", "arbitrary"),
        name="ssm_conv_proj",
    )(h, w_in, conv_w.astype(F32), conv_b.astype(F32)[None, :])


def _ssd_kernel(zs_ref, xs_ref, b_ref, c_ref, acum_t_ref, acum_c_ref, dt_c_ref, ex_ref,
                d_ref, ng_ref, y_ref, state_ref, *, chunk, rpg, groups):
    c = pl.program_id(1)
    L, H, N = chunk, chunk // 2, SSM_STATE
    gw = rpg * HEAD_DIM
    tri = (lax.broadcasted_iota(jnp.int32, (H, H), 0) >= lax.broadcasted_iota(jnp.int32, (H, H), 1))

    @pl.when(c == 0)
    def _():
        state_ref[...] = jnp.zeros(state_ref.shape, F32)

    acum_c = acum_c_ref[...]
    dt_c = dt_c_ref[...]
    grow = jnp.exp2(acum_c)
    to_end = dt_c * jnp.exp2(acum_c[L - 1:L, :] - acum_c)
    stack = lambda v: jnp.concatenate(_bf16_terms(v, 2), axis=1).astype(BF16)
    dt_s, grow_s, to_end_s = stack(dt_c), stack(grow), stack(to_end)

    for u in range(groups):
        lanes = slice(u * gw, (u + 1) * gw)
        xs = xs_ref[:, lanes].astype(F32)
        bm = b_ref[:, u * N:(u + 1) * N]
        cm = c_ref[:, u * N:(u + 1) * N]
        ex = ex_ref[u]
        dt_full = _dot(dt_s, ex)
        grow_full = _dot(grow_s, ex)
        to_end_full = _dot(to_end_s, ex)
        xdt_b = (xs * dt_full).astype(BF16)
        cb = _dot_nt(cm, bm)
        y_parts = []
        for r in range(rpg):
            head = u * rpg + r
            col = acum_c[:, head:head + 1]
            row = acum_t_ref[head:head + 1, :]
            d00 = jnp.exp2(jnp.where(tri, col[:H] - row[:, :H], -jnp.inf))
            d11 = jnp.exp2(jnp.where(tri, col[H:] - row[:, H:], -jnp.inf))
            d10 = jnp.exp2(col[H:] - row[:, :H])
            m00 = (cb[:H, :H] * d00).astype(BF16)
            m1 = jnp.concatenate([cb[H:, :H] * d10, cb[H:, H:] * d11], axis=1).astype(BF16)
            x_r = xdt_b[:, r * HEAD_DIM:(r + 1) * HEAD_DIM]
            y_parts.append(jnp.concatenate([_dot(m00, x_r[:H]), _dot(m1, x_r)], axis=0))
        y = jnp.concatenate(y_parts, axis=1)

        state = state_ref[u]
        y = y + _dot(cm, state.astype(BF16)) * grow_full
        state_ref[u] = state * grow_full[L - 1:L, :] + _dot_tn(bm, (xs * to_end_full).astype(BF16))

        y = (y + d_ref[:, lanes] * xs) * zs_ref[:, lanes].astype(F32)
        y = y * lax.rsqrt(jnp.mean(y * y, axis=-1, keepdims=True) + EPS)
        y_ref[:, lanes] = (y * ng_ref[:, lanes]).astype(BF16)


def _ssd(zs, xbc, acum_t, acum_c, dt_c, d_skip, norm_g, *, batch):
    t, d_inner = zs.shape
    heads = acum_t.shape[0]
    G, N = SSM_GROUPS, SSM_STATE
    rpg = heads // G
    gw = d_inner // G
    L = SSM_CHUNK
    nc = t // batch // L
    rows = lambda b, c: (b * nc + c, 0)
    fixed = lambda b, c: (0, 0)
    lane_head = jnp.arange(G)[:, None, None] * rpg + jnp.arange(gw)[None, None, :] // HEAD_DIM
    one_hot = (lane_head == jnp.arange(heads)[None, :, None]).astype(BF16)
    rep = lambda v: jnp.repeat(v.astype(F32), HEAD_DIM)[None, :]
    bc0 = d_inner // (G * N)
    return pl.pallas_call(
        functools.partial(_ssd_kernel, chunk=L, rpg=rpg, groups=G),
        out_shape=jax.ShapeDtypeStruct((t, d_inner), BF16),
        grid=(batch, nc),
        in_specs=[pl.BlockSpec((L, d_inner), rows),
                  pl.BlockSpec((L, d_inner), rows),
                  pl.BlockSpec((L, G * N), lambda b, c: (b * nc + c, bc0)),
                  pl.BlockSpec((L, G * N), lambda b, c: (b * nc + c, bc0 + 1)),
                  pl.BlockSpec((heads, L), lambda b, c: (0, b * nc + c)),
                  pl.BlockSpec((L, heads), rows),
                  pl.BlockSpec((L, heads), rows),
                  pl.BlockSpec((G, 2 * heads, gw), lambda b, c: (0, 0, 0)),
                  pl.BlockSpec((1, d_inner), fixed),
                  pl.BlockSpec((1, d_inner), fixed)],
        out_specs=pl.BlockSpec((L, d_inner), rows),
        scratch_shapes=[pltpu.VMEM((G, N, gw), F32)],
        compiler_params=_params("arbitrary", "arbitrary"),
        name="ssd_scan",
    )(zs, xbc, xbc, xbc, acum_t, acum_c, dt_c, jnp.tile(one_hot, (1, 2, 1)),
      rep(d_skip), norm_g.astype(F32)[None, :])


def kernel(x, positions, mixer_norm, ffn_norm, attn_w_qkv, attn_q_norm, attn_k_norm, attn_sinks,
           attn_w_o, ssm_w_in, ssm_conv_w, ssm_conv_b, ssm_dt_bias, ssm_a_log, ssm_d, ssm_norm,
           ssm_w_out, ffn_w_gate, ffn_w_up, ffn_w_down):
    batch, seq, d = x.shape
    t = batch * seq
    xf = x.reshape(t, d)
    pos = positions.reshape(t, 1)
    d_inner = ssm_w_out.shape[1]
    main_w = 2 * d_inner + 2 * SSM_GROUPS * SSM_STATE
    bm = _pick(t, (512, 256, 128))
    bm_big = _pick(seq, (1024, 512, 256, 128))
    bf = _pick(ffn_w_gate.shape[2], (512, 256))
    bn = _pick(d, (1024, 512))

    def ffn(xf, i):
        return _ffn(xf, ffn_norm[i], w_gate, w_up, w_down, layer=i, bm=bm_big, bf=bf)

    q, k, v, w_down = _qkv_proj(xf, pos, mixer_norm[0], attn_w_qkv[0].astype(BF16),
                                attn_q_norm[0], attn_k_norm[0], ffn_w_down, bm=bm)
    att, w_gate, w_up = _attention(q, k, v, attn_sinks[0], ffn_w_gate, ffn_w_up, batch=batch)
    xf = _proj_residual(xf, att, attn_w_o[0].astype(BF16), bm=bm_big, bn=bn)
    xf = ffn(xf, 0)

    w_in = ssm_w_in[0].astype(BF16)
    bn_in = _pick(SSM_GROUPS * SSM_STATE, (1024, 512, 256))
    zs, h, acum_t, acum_c, dt_c = _ssm_gate_proj(
        xf, mixer_norm[1], w_in, w_in[:, main_w:], ssm_dt_bias[0], ssm_a_log[0],
        bm=bm_big, bn=bn_in, n=d_inner)
    xbc = _ssm_conv_proj(h, w_in, ssm_conv_w[0], ssm_conv_b[0],
                         bm=bm_big, bn=bn_in, seq=seq, col0=d_inner // bn_in, n=main_w - d_inner)
    y = _ssd(zs, xbc, acum_t, acum_c, dt_c, ssm_d[0], ssm_norm[0], batch=batch)
    xf = _proj_residual(xf, y, ssm_w_out[0].astype(BF16), bm=bm, bn=bn)
    xf = ffn(xf, 1)
    return xf.reshape(batch, seq, d)
```

```python
import functools

import jax
import jax.numpy as jnp
from jax import lax
from jax.experimental import pallas as pl
from jax.experimental.pallas import tpu as pltpu

EPS = 1e-6
HEAD_DIM = 64
ATT_GROUP = 8
WINDOW = 128
ROPE_THETA = 10000.0
SSM_GROUPS = 8
SSM_STATE = 128
SSM_CONV = 4
SSM_CHUNK = 256
LOG2_E = 1.4426950408889634
LANES = 128
VMEM_LIMIT_BYTES = 56 * 1024 * 1024

F32 = jnp.float32
BF16 = jnp.bfloat16


def _params(*sem):
    return pltpu.CompilerParams(dimension_semantics=sem, vmem_limit_bytes=VMEM_LIMIT_BYTES)


def _pick(n, candidates):
    for c in candidates:
        if n % c == 0:
            return c
    raise ValueError(f"no tile in {candidates} divides {n}")


def _ride_along_cast(w, steps, step_index):
    layers, rows, cols = w.shape
    slab = layers * rows // steps
    per_layer = rows // slab
    assert slab * steps == layers * rows and per_layer * slab == rows and slab % 16 == 0
    imap = lambda *g: (step_index(*g) // per_layer, step_index(*g) % per_layer, 0)
    spec = pl.BlockSpec((None, slab, cols), imap)
    return spec, jax.ShapeDtypeStruct(w.shape, BF16)


def _rms_norm_bf16(x, gain):
    y = x * lax.rsqrt(jnp.mean(x * x, axis=-1, keepdims=True) + EPS)
    return (y * gain).astype(BF16)


def _dot(a, b):
    return jnp.dot(a, b, preferred_element_type=F32)


def _dot_nt(a, b):
    return lax.dot_general(a, b, (((1,), (1,)), ((), ())), preferred_element_type=F32)


def _dot_tn(a, b):
    return lax.dot_general(a, b, (((0,), (0,)), ((), ())), preferred_element_type=F32)


def _qkv_kernel(x_ref, pos_ref, g_ref, w_ref, invf_ref, qg_ref, kg_ref, side_a_ref, side_b_ref,
                q_ref, k_ref, v_ref, side_a16_ref, side_b16_ref, acc_ref, *, dq, dk, cw):
    side_a16_ref[...] = side_a_ref[...].astype(BF16)
    side_b16_ref[...] = side_b_ref[...].astype(BF16)
    h = _rms_norm_bf16(x_ref[...], g_ref[...])
    ang = pos_ref[...].astype(F32) * invf_ref[...]
    lane = lax.broadcasted_iota(jnp.int32, (1, LANES), 1)
    first_half = (lane % HEAD_DIM) < (HEAD_DIM // 2)
    cos = jnp.cos(ang)
    sin = jnp.sin(ang)
    sin_signed = jnp.where(first_half, -sin, sin)

    r_i = lax.broadcasted_iota(jnp.int32, (cw, cw), 0)
    c_i = lax.broadcasted_iota(jnp.int32, (cw, cw), 1)
    same_head = (r_i // HEAD_DIM == c_i // HEAD_DIM).astype(BF16)

    def norm_rope(a, gain, scale):
        sq = a * a
        sq_hi = sq.astype(BF16)
        sq_lo = (sq - sq_hi.astype(F32)).astype(BF16)
        ss = _dot(sq_hi, same_head) + _dot(sq_lo, same_head)
        outs = []
        for s0 in range(0, cw, LANES):
            sl = slice(s0, s0 + LANES)
            y = a[:, sl] * lax.rsqrt(ss[:, sl] * (1.0 / HEAD_DIM) + EPS) * gain
            partner = jnp.where(first_half,
                                pltpu.roll(y, LANES - HEAD_DIM // 2, 1),
                                pltpu.roll(y, HEAD_DIM // 2, 1))
            outs.append(((y * cos + partner * sin_signed) * scale).astype(BF16))
        return jnp.concatenate(outs, axis=1)

    q_scale = HEAD_DIM ** -0.5
    acc_ref[...] = _dot(h, w_ref[:, :dq + dk])
    v_ref[...] = _dot(h, w_ref[:, dq + dk:]).astype(BF16)
    for c0 in range(0, dq + dk, cw):
        acc = acc_ref[:, c0:c0 + cw]
        if c0 < dq:
            q_ref[:, c0:c0 + cw] = norm_rope(acc, qg_ref[...], q_scale)
        else:
            k_ref[:, c0 - dq:c0 - dq + cw] = norm_rope(acc, kg_ref[...], 1.0)


def _qkv_proj(x, pos, gain, w, q_gain, k_gain, side_a, side_b, *, bm):
    t, d = x.shape
    dq = d
    dk = d // ATT_GROUP
    cw = _pick(dk, (256, 128))
    inv_freq = ROPE_THETA ** (-jnp.arange(0, HEAD_DIM, 2, dtype=F32) / HEAD_DIM)
    invf = jnp.tile(inv_freq, LANES // (HEAD_DIM // 2))[None, :]
    tile_gain = lambda g: jnp.tile(g.astype(F32), LANES // HEAD_DIM)[None, :]
    row = lambda i: (i, 0)
    fixed = lambda i: (0, 0)
    spec_a, shape_a = _ride_along_cast(side_a, t // bm, lambda i: i)
    spec_b, shape_b = _ride_along_cast(side_b, t // bm, lambda i: i)
    return pl.pallas_call(
        functools.partial(_qkv_kernel, dq=dq, dk=dk, cw=cw),
        out_shape=(jax.ShapeDtypeStruct((t, dq), BF16),
                   jax.ShapeDtypeStruct((t, dk), BF16),
                   jax.ShapeDtypeStruct((t, dk), BF16),
                   shape_a, shape_b),
        grid=(t // bm,),
        in_specs=[pl.BlockSpec((bm, d), row),
                  pl.BlockSpec((bm, 1), row),
                  pl.BlockSpec((1, d), fixed),
                  pl.BlockSpec((d, dq + 2 * dk), fixed),
                  pl.BlockSpec((1, LANES), fixed),
                  pl.BlockSpec((1, LANES), fixed),
                  pl.BlockSpec((1, LANES), fixed),
                  spec_a, spec_b],
        out_specs=(pl.BlockSpec((bm, dq), row),
                   pl.BlockSpec((bm, dk), row),
                   pl.BlockSpec((bm, dk), row),
                   spec_a, spec_b),
        scratch_shapes=[pltpu.VMEM((bm, dq + dk), F32)],
        compiler_params=_params("arbitrary"),
        name="qkv_proj",
    )(x, pos, gain[None, :], w, invf, tile_gain(q_gain), tile_gain(k_gain), side_a, side_b)


def _attn_kernel(sink_ref, q_ref, kp_ref, ko_ref, vp_ref, vo_ref, side_a_ref, side_b_ref, side_c_ref,
                 o_ref, side_a16_ref, side_b16_ref, side_c16_ref, s_ref, *, n_kv):
    side_a16_ref[...] = side_a_ref[...].astype(BF16)
    side_b16_ref[...] = side_b_ref[...].astype(BF16)
    side_c16_ref[...] = side_c_ref[...].astype(BF16)
    n = pl.program_id(1)
    W, pairs, slab = WINDOW, ATT_GROUP // 2, 2 * HEAD_DIM
    qi = lax.broadcasted_iota(jnp.int32, (W, 2 * W), 0)
    kj = lax.broadcasted_iota(jnp.int32, (W, 2 * W), 1)
    lag = qi + W - kj
    first_key = jnp.where(n > 0, 0, W)
    valid = ((lag >= 0) & (lag < W) & (kj >= first_key))[None]
    sink_col = kj[:1] == 0
    key_row = lax.broadcasted_iota(jnp.int32, (2 * W, 1), 0)
    zeros = jnp.zeros((2 * W, HEAD_DIM), BF16)
    ones = jnp.ones((2 * W, HEAD_DIM), BF16)
    v_wide = []
    for kv in range(n_kv):
        ks = slice(kv * HEAD_DIM, (kv + 1) * HEAD_DIM)
        k = jnp.concatenate([kp_ref[:, ks], ko_ref[:, ks]], axis=0)
        v = jnp.concatenate([vp_ref[:, ks], vo_ref[:, ks]], axis=0)
        v = jnp.where(key_row == 0, jnp.zeros_like(v), v)
        v_wide.append((jnp.concatenate([v, zeros, ones, zeros], axis=1),
                       jnp.concatenate([zeros, v, zeros, ones], axis=1)))
        c0 = kv * ATT_GROUP * HEAD_DIM
        qp = jnp.concatenate([q_ref[:, c0 + p * slab:c0 + (p + 1) * slab] for p in range(pairs)],
                             axis=0)
        s_ref[2 * kv] = _dot_nt(qp, jnp.concatenate([k, zeros], axis=1))
        s_ref[2 * kv + 1] = _dot_nt(qp, jnp.concatenate([zeros, k], axis=1))

    for kv in range(n_kv):
        c0 = kv * ATT_GROUP * HEAD_DIM

        def probs(parity):
            sink = jnp.concatenate(
                [jnp.full((1, 1, 1), sink_ref[kv * ATT_GROUP + 2 * p + parity], F32)
                 for p in range(pairs)], axis=0)
            fill = jnp.where(sink_col[None], sink, -jnp.inf)
            s = jnp.where(valid, s_ref[2 * kv + parity].reshape(pairs, W, 2 * W), fill)
            m = jnp.max(s, axis=-1, keepdims=True)
            return jnp.exp(s - m).astype(BF16).reshape(pairs * W, 2 * W)

        nd = _dot(probs(0), v_wide[kv][0]) + _dot(probs(1), v_wide[kv][1])
        out = (nd[:, :slab] / nd[:, slab:]).astype(BF16)
        for p in range(pairs):
            o_ref[:, c0 + p * slab:c0 + (p + 1) * slab] = out[p * W:(p + 1) * W]


def _attention(q, k, v, sinks, side_a, side_b, side_c, *, batch):
    t, dq = q.shape
    dk = k.shape[1]
    nb = t // batch // WINDOW
    own = lambda b, n: (b * nb + n, 0)
    prev = lambda b, n: (b * nb + jnp.maximum(n - 1, 0), 0)
    step = lambda b, n: b * nb + n
    spec_a, shape_a = _ride_along_cast(side_a, batch * nb, step)
    spec_b, shape_b = _ride_along_cast(side_b, batch * nb, step)
    spec_c, shape_c = _ride_along_cast(side_c, batch * nb, step)
    return pl.pallas_call(
        functools.partial(_attn_kernel, n_kv=dk // HEAD_DIM),
        out_shape=(jax.ShapeDtypeStruct((t, dq), BF16), shape_a, shape_b, shape_c),
        grid=(batch, nb),
        in_specs=[pl.BlockSpec(memory_space=pltpu.SMEM),
                  pl.BlockSpec((WINDOW, dq), own),
                  pl.BlockSpec((WINDOW, dk), prev),
                  pl.BlockSpec((WINDOW, dk), own),
                  pl.BlockSpec((WINDOW, dk), prev),
                  pl.BlockSpec((WINDOW, dk), own),
                  spec_a, spec_b, spec_c],
        out_specs=(pl.BlockSpec((WINDOW, dq), own), spec_a, spec_b, spec_c),
        scratch_shapes=[pltpu.VMEM((2 * (dk // HEAD_DIM), ATT_GROUP // 2 * WINDOW, 2 * WINDOW), F32)],
        compiler_params=_params("arbitrary", "arbitrary"),
        name="swa_attention",
    )(sinks.astype(F32), q, k, k, v, v, side_a, side_b, side_c)


def _proj_residual_kernel(x_ref, a_ref, w_ref, o_ref):
    o_ref[...] = x_ref[...] + _dot(a_ref[...], w_ref[...])


def _proj_residual(x, a, w, *, bm, bn):
    t, d = x.shape
    k = a.shape[1]
    return pl.pallas_call(
        _proj_residual_kernel,
        out_shape=jax.ShapeDtypeStruct((t, d), F32),
        grid=(d // bn, t // bm),
        in_specs=[pl.BlockSpec((bm, bn), lambda j, i: (i, j)),
                  pl.BlockSpec((bm, k), lambda j, i: (i, 0)),
                  pl.BlockSpec((k, bn), lambda j, i: (0, j))],
        out_specs=pl.BlockSpec((bm, bn), lambda j, i: (i, j)),
        compiler_params=_params("arbitrary", "arbitrary"),
        name="proj_residual",
    )(x, a, w)


def _ffn_kernel(x_ref, g_ref, wg_ref, wu_ref, wd_ref, o_ref, h_ref):
    f = pl.program_id(1)

    @pl.when(f == 0)
    def _():
        x = x_ref[...]
        h_ref[...] = _rms_norm_bf16(x, g_ref[...])
        o_ref[...] = x

    h = h_ref[...]
    gate = _dot(h, wg_ref[...])
    up = _dot(h, wu_ref[...])
    act = (gate * jax.nn.sigmoid(gate) * up).astype(BF16)
    o_ref[...] += _dot(act, wd_ref[...])


def _ffn(x, gain, w_gate, w_up, w_down, *, layer, bm, bf):
    t, d = x.shape
    dff = w_gate.shape[2]
    return pl.pallas_call(
        _ffn_kernel,
        out_shape=jax.ShapeDtypeStruct((t, d), F32),
        grid=(t // bm, dff // bf),
        in_specs=[pl.BlockSpec((bm, d), lambda i, f: (i, 0)),
                  pl.BlockSpec((1, d), lambda i, f: (0, 0)),
                  pl.BlockSpec((None, d, bf), lambda i, f: (layer, 0, f)),
                  pl.BlockSpec((None, d, bf), lambda i, f: (layer, 0, f)),
                  pl.BlockSpec((None, bf, d), lambda i, f: (layer, f, 0))],
        out_specs=pl.BlockSpec((bm, d), lambda i, f: (i, 0)),
        scratch_shapes=[pltpu.VMEM((bm, d), BF16)],
        compiler_params=_params("arbitrary", "arbitrary"),
        name="swiglu_ffn",
    )(x, gain[None, :], w_gate, w_up, w_down)


def _softplus(x):
    return jnp.maximum(x, 0.0) + jnp.log1p(jnp.exp(-jnp.abs(x)))


def _silu(x):
    return x * jax.nn.sigmoid(x)


def _bf16_terms(x, n):
    terms = []
    for _ in range(n):
        t = x.astype(BF16).astype(F32)
        terms.append(t)
        x = x - t
    return terms


def _ssm_gate_kernel(x_ref, g_ref, w_ref, wdtt_ref, dtb_ref, alog_row_ref, alog_col_ref,
                     zs_ref, h_ref, acum_t_ref, acum_c_ref, dt_c_ref, *, chunk):
    @pl.when(pl.program_id(1) == 0)
    def _():
        h = _rms_norm_bf16(x_ref[...], g_ref[...])
        h_ref[...] = h
        heads = wdtt_ref.shape[0]
        L = chunk
        dt_t = _softplus(_dot_nt(wdtt_ref[...], h) + dtb_ref[...])
        dt_c = dt_t.T
        dta_c = dt_c * (-LOG2_E * jnp.exp(alog_row_ref[...]))
        dta_t = dt_t * (-LOG2_E * jnp.exp(alog_col_ref[...]))
        r_i = lax.broadcasted_iota(jnp.int32, (L, L), 0)
        c_i = lax.broadcasted_iota(jnp.int32, (L, L), 1)
        upper = (r_i <= c_i).astype(BF16)
        lower = (r_i >= c_i).astype(BF16)
        for r0 in range(0, x_ref.shape[0], L):
            rows = slice(r0, r0 + L)
            stacked = jnp.concatenate(_bf16_terms(dta_t[:, rows], 3), axis=0).astype(BF16)
            sums = _dot(stacked, upper)
            acum_t_ref[:, rows] = sums[:heads] + sums[heads:2 * heads] + sums[2 * heads:]
            acum_c = sum(_dot(lower, term.astype(BF16)) for term in _bf16_terms(dta_c[rows], 3))
            acum_c_ref[rows, :] = acum_c
        dt_c_ref[...] = dt_c

    zs_ref[...] = _silu(_dot(h_ref[...], w_ref[...])).astype(BF16)


def _ssm_gate_proj(x, gain, w_in, w_dt, dt_bias, a_log, *, bm, bn, n):
    t, d = x.shape
    heads = w_dt.shape[1]
    fixed = lambda i, j: (0, 0)
    row_tile = lambda i, j: (i, 0)
    f32 = lambda v: v.astype(F32)
    return pl.pallas_call(
        functools.partial(_ssm_gate_kernel, chunk=SSM_CHUNK),
        out_shape=(jax.ShapeDtypeStruct((t, n), BF16),
                   jax.ShapeDtypeStruct((t, d), BF16),
                   jax.ShapeDtypeStruct((heads, t), F32),
                   jax.ShapeDtypeStruct((t, heads), F32),
                   jax.ShapeDtypeStruct((t, heads), F32)),
        grid=(t // bm, n // bn),
        in_specs=[pl.BlockSpec((bm, d), row_tile),
                  pl.BlockSpec((1, d), fixed),
                  pl.BlockSpec((d, bn), lambda i, j: (0, j)),
                  pl.BlockSpec((heads, d), fixed),
                  pl.BlockSpec((heads, 1), fixed),
                  pl.BlockSpec((1, heads), fixed),
                  pl.BlockSpec((heads, 1), fixed)],
        out_specs=(pl.BlockSpec((bm, bn), lambda i, j: (i, j)),
                   pl.BlockSpec((bm, d), row_tile),
                   pl.BlockSpec((heads, bm), lambda i, j: (0, i)),
                   pl.BlockSpec((bm, heads), row_tile),
                   pl.BlockSpec((bm, heads), row_tile)),
        compiler_params=_params("arbitrary", "arbitrary"),
        name="ssm_gate_proj",
    )(x, gain[None, :], w_in, w_dt.T, f32(dt_bias)[:, None], f32(a_log)[None, :], f32(a_log)[:, None])


def _ssm_conv_kernel(h_ref, w_ref, cw_ref, cb_ref, o_ref, acc_ref, carry_ref, *, tiles_per_seq):
    i = pl.program_id(0)
    j = pl.program_id(1)
    bm = h_ref.shape[0]
    acc_ref[...] = _dot(h_ref[...], w_ref[...])
    acc = acc_ref[...]
    row8 = lax.broadcasted_iota(jnp.int32, (8, 1), 0)
    tail = jnp.where(i % tiles_per_seq != 0, carry_ref[j], 0.0)
    carry_ref[j] = acc[bm - 8:, :]
    def shift_rows(v, v_tail, back):
        rolled = pltpu.roll(v, back, 0)
        head = jnp.where(row8 < back, pltpu.roll(v_tail, back, 0), rolled[:8])
        return jnp.concatenate([head, rolled[8:]], axis=0)

    w0, w1, w2, w3 = (cw_ref[k:k + 1, :] for k in range(SSM_CONV))
    a1 = shift_rows(acc, tail, 1)
    q = w1 * acc + w0 * a1
    q_tail = w1 * tail + w0 * pltpu.roll(tail, 1, 0)
    out = cb_ref[...] + w3 * acc + w2 * a1 + shift_rows(q, q_tail, 2)
    o_ref[...] = _silu(out).astype(BF16)


def _ssm_conv_proj(h, w_in, conv_w, conv_b, *, bm, bn, seq, col0, n):
    t, d = h.shape
    return pl.pallas_call(
        functools.partial(_ssm_conv_kernel, tiles_per_seq=seq // bm),
        out_shape=jax.ShapeDtypeStruct((t, n), BF16),
        grid=(t // bm, n // bn),
        in_specs=[pl.BlockSpec((bm, d), lambda i, j: (i, 0)),
                  pl.BlockSpec((d, bn), lambda i, j: (0, col0 + j)),
                  pl.BlockSpec((SSM_CONV, bn), lambda i, j: (0, j)),
                  pl.BlockSpec((1, bn), lambda i, j: (0, j))],
        out_specs=pl.BlockSpec((bm, bn), lambda i, j: (i, j)),
        scratch_shapes=[pltpu.VMEM((bm, bn), F32), pltpu.VMEM((n // bn, 8, bn), F32)],
        compiler_params=_params("arbitrary", "arbitrary"),
        name="ssm_conv_proj",
    )(h, w_in, conv_w.astype(F32), conv_b.astype(F32)[None, :])


def _ssd_kernel(zs_ref, xs_ref, b_ref, c_ref, acum_t_ref, acum_c_ref, dt_c_ref, ex_ref,
                d_ref, ng_ref, y_ref, state_ref, *, chunk, rpg, groups):
    c = pl.program_id(1)
    L, H, N = chunk, chunk // 2, SSM_STATE
    gw = rpg * HEAD_DIM
    tri = (lax.broadcasted_iota(jnp.int32, (H, H), 0) >= lax.broadcasted_iota(jnp.int32, (H, H), 1))

    @pl.when(c == 0)
    def _():
        state_ref[...] = jnp.zeros(state_ref.shape, F32)

    acum_c = acum_c_ref[...]
    dt_c = dt_c_ref[...]
    grow = jnp.exp2(acum_c)
    to_end = dt_c * jnp.exp2(acum_c[L - 1:L, :] - acum_c)
    stack = lambda v: jnp.concatenate(_bf16_terms(v, 2), axis=1).astype(BF16)
    dt_s, grow_s, to_end_s = stack(dt_c), stack(grow), stack(to_end)

    for u in range(groups):
        lanes = slice(u * gw, (u + 1) * gw)
        xs = xs_ref[:, lanes].astype(F32)
        bm = b_ref[:, u * N:(u + 1) * N]
        cm = c_ref[:, u * N:(u + 1) * N]
        ex = ex_ref[u]
        dt_full = _dot(dt_s, ex)
        grow_full = _dot(grow_s, ex)
        to_end_full = _dot(to_end_s, ex)
        xdt_b = (xs * dt_full).astype(BF16)
        cb = _dot_nt(cm, bm)
        y_parts = []
        for r in range(rpg):
            head = u * rpg + r
            col = acum_c[:, head:head + 1]
            row = acum_t_ref[head:head + 1, :]
            d00 = jnp.exp2(jnp.where(tri, col[:H] - row[:, :H], -jnp.inf))
            d11 = jnp.exp2(jnp.where(tri, col[H:] - row[:, H:], -jnp.inf))
            d10 = jnp.exp2(col[H:] - row[:, :H])
            m00 = (cb[:H, :H] * d00).astype(BF16)
            m1 = jnp.concatenate([cb[H:, :H] * d10, cb[H:, H:] * d11], axis=1).astype(BF16)
            x_r = xdt_b[:, r * HEAD_DIM:(r + 1) * HEAD_DIM]
            y_parts.append(jnp.concatenate([_dot(m00, x_r[:H]), _dot(m1, x_r)], axis=0))
        y = jnp.concatenate(y_parts, axis=1)

        state = state_ref[u]
        y = y + _dot(cm, state.astype(BF16)) * grow_full
        state_ref[u] = state * grow_full[L - 1:L, :] + _dot_tn(bm, (xs * to_end_full).astype(BF16))

        y = (y + d_ref[:, lanes] * xs) * zs_ref[:, lanes].astype(F32)
        y = y * lax.rsqrt(jnp.mean(y * y, axis=-1, keepdims=True) + EPS)
        y_ref[:, lanes] = (y * ng_ref[:, lanes]).astype(BF16)


def _ssd(zs, xbc, acum_t, acum_c, dt_c, d_skip, norm_g, *, batch):
    t, d_inner = zs.shape
    heads = acum_t.shape[0]
    G, N = SSM_GROUPS, SSM_STATE
    rpg = heads // G
    gw = d_inner // G
    L = SSM_CHUNK
    nc = t // batch // L
    rows = lambda b, c: (b * nc + c, 0)
    fixed = lambda b, c: (0, 0)
    lane_head = jnp.arange(G)[:, None, None] * rpg + jnp.arange(gw)[None, None, :] // HEAD_DIM
    one_hot = (lane_head == jnp.arange(heads)[None, :, None]).astype(BF16)
    rep = lambda v: jnp.repeat(v.astype(F32), HEAD_DIM)[None, :]
    bc0 = d_inner // (G * N)
    return pl.pallas_call(
        functools.partial(_ssd_kernel, chunk=L, rpg=rpg, groups=G),
        out_shape=jax.ShapeDtypeStruct((t, d_inner), BF16),
        grid=(batch, nc),
        in_specs=[pl.BlockSpec((L, d_inner), rows),
                  pl.BlockSpec((L, d_inner), rows),
                  pl.BlockSpec((L, G * N), lambda b, c: (b * nc + c, bc0)),
                  pl.BlockSpec((L, G * N), lambda b, c: (b * nc + c, bc0 + 1)),
                  pl.BlockSpec((heads, L), lambda b, c: (0, b * nc + c)),
                  pl.BlockSpec((L, heads), rows),
                  pl.BlockSpec((L, heads), rows),
                  pl.BlockSpec((G, 2 * heads, gw), lambda b, c: (0, 0, 0)),
                  pl.BlockSpec((1, d_inner), fixed),
                  pl.BlockSpec((1, d_inner), fixed)],
        out_specs=pl.BlockSpec((L, d_inner), rows),
        scratch_shapes=[pltpu.VMEM((G, N, gw), F32)],
        compiler_params=_params("arbitrary", "arbitrary"),
        name="ssd_scan",
    )(zs, xbc, xbc, xbc, acum_t, acum_c, dt_c, jnp.tile(one_hot, (1, 2, 1)),
      rep(d_skip), norm_g.astype(F32)[None, :])


def kernel(x, positions, mixer_norm, ffn_norm, attn_w_qkv, attn_q_norm, attn_k_norm, attn_sinks,
           attn_w_o, ssm_w_in, ssm_conv_w, ssm_conv_b, ssm_dt_bias, ssm_a_log, ssm_d, ssm_norm,
           ssm_w_out, ffn_w_gate, ffn_w_up, ffn_w_down):
    batch, seq, d = x.shape
    t = batch * seq
    xf = x.reshape(t, d)
    pos = positions.reshape(t, 1)
    d_inner = ssm_w_out.shape[1]
    main_w = 2 * d_inner + 2 * SSM_GROUPS * SSM_STATE
    bm = _pick(t, (512, 256, 128))
    bm_big = _pick(seq, (1024, 512, 256, 128))
    bf = _pick(ffn_w_gate.shape[2], (512, 256))
    bn = _pick(d, (1024, 512))

    def ffn(xf, i):
        return _ffn(xf, ffn_norm[i], w_gate, w_up, w_down, layer=i, bm=bm_big, bf=bf)

    q, k, v, w_down, w_o = _qkv_proj(xf, pos, mixer_norm[0], attn_w_qkv[0].astype(BF16),
                                     attn_q_norm[0], attn_k_norm[0], ffn_w_down, attn_w_o, bm=bm)
    att, w_gate, w_up, w_out = _attention(q, k, v, attn_sinks[0], ffn_w_gate, ffn_w_up, ssm_w_out,
                                          batch=batch)
    xf = _proj_residual(xf, att, w_o[0], bm=bm_big, bn=bn)
    xf = ffn(xf, 0)

    w_in = ssm_w_in[0].astype(BF16)
    bn_in = _pick(SSM_GROUPS * SSM_STATE, (1024, 512, 256))
    zs, h, acum_t, acum_c, dt_c = _ssm_gate_proj(
        xf, mixer_norm[1], w_in, w_in[:, main_w:], ssm_dt_bias[0], ssm_a_log[0],
        bm=bm_big, bn=bn_in, n=d_inner)
    xbc = _ssm_conv_proj(h, w_in, ssm_conv_w[0], ssm_conv_b[0],
                         bm=bm_big, bn=bn_in, seq=seq, col0=d_inner // bn_in, n=main_w - d_inner)
    y = _ssd(zs, xbc, acum_t, acum_c, dt_c, ssm_d[0], ssm_norm[0], batch=batch)
    xf = _proj_residual(xf, y, w_out[0], bm=bm, bn=bn)
    xf = ffn(xf, 1)
    return xf.reshape(batch, seq, d)
```

```python
import functools

import jax
import jax.numpy as jnp
from jax import lax
from jax.experimental import pallas as pl
from jax.experimental.pallas import tpu as pltpu

EPS = 1e-6
HEAD_DIM = 64
ATT_GROUP = 8
WINDOW = 128
ROPE_THETA = 10000.0
SSM_GROUPS = 8
SSM_STATE = 128
SSM_CONV = 4
SSM_CHUNK = 256
LOG2_E = 1.4426950408889634
LANES = 128
VMEM_LIMIT_BYTES = 56 * 1024 * 1024

F32 = jnp.float32
BF16 = jnp.bfloat16


def _params(*sem):
    return pltpu.CompilerParams(dimension_semantics=sem, vmem_limit_bytes=VMEM_LIMIT_BYTES)


def _pick(n, candidates):
    for c in candidates:
        if n % c == 0:
            return c
    raise ValueError(f"no tile in {candidates} divides {n}")


def _ride_along_cast(w, steps, step_index):
    layers, rows, cols = w.shape
    slab = layers * rows // steps
    per_layer = rows // slab
    assert slab * steps == layers * rows and per_layer * slab == rows and slab % 16 == 0
    imap = lambda *g: (step_index(*g) // per_layer, step_index(*g) % per_layer, 0)
    spec = pl.BlockSpec((None, slab, cols), imap)
    return spec, jax.ShapeDtypeStruct(w.shape, BF16)


def _rms_norm_bf16(x, gain):
    y = x * lax.rsqrt(jnp.mean(x * x, axis=-1, keepdims=True) + EPS)
    return (y * gain).astype(BF16)


def _dot(a, b):
    return jnp.dot(a, b, preferred_element_type=F32)


def _dot_nt(a, b):
    return lax.dot_general(a, b, (((1,), (1,)), ((), ())), preferred_element_type=F32)


def _dot_tn(a, b):
    return lax.dot_general(a, b, (((0,), (0,)), ((), ())), preferred_element_type=F32)


def _qkv_kernel(x_ref, pos_ref, g_ref, w_ref, invf_ref, qg_ref, kg_ref, side_a_ref, side_b_ref,
                q_ref, k_ref, v_ref, side_a16_ref, side_b16_ref, acc_ref, *, dq, dk, cw):
    side_a16_ref[...] = side_a_ref[...].astype(BF16)
    side_b16_ref[...] = side_b_ref[...].astype(BF16)
    h = _rms_norm_bf16(x_ref[...], g_ref[...])
    ang = pos_ref[...].astype(F32) * invf_ref[...]
    lane = lax.broadcasted_iota(jnp.int32, (1, LANES), 1)
    first_half = (lane % HEAD_DIM) < (HEAD_DIM // 2)
    cos = jnp.cos(ang)
    sin = jnp.sin(ang)
    sin_signed = jnp.where(first_half, -sin, sin)

    r_i = lax.broadcasted_iota(jnp.int32, (cw, cw), 0)
    c_i = lax.broadcasted_iota(jnp.int32, (cw, cw), 1)
    same_head = (r_i // HEAD_DIM == c_i // HEAD_DIM).astype(BF16)

    def norm_rope(a, gain, scale):
        sq = a * a
        sq_hi = sq.astype(BF16)
        sq_lo = (sq - sq_hi.astype(F32)).astype(BF16)
        ss = _dot(sq_hi, same_head) + _dot(sq_lo, same_head)
        outs = []
        for s0 in range(0, cw, LANES):
            sl = slice(s0, s0 + LANES)
            y = a[:, sl] * lax.rsqrt(ss[:, sl] * (1.0 / HEAD_DIM) + EPS) * gain
            partner = jnp.where(first_half,
                                pltpu.roll(y, LANES - HEAD_DIM // 2, 1),
                                pltpu.roll(y, HEAD_DIM // 2, 1))
            outs.append(((y * cos + partner * sin_signed) * scale).astype(BF16))
        return jnp.concatenate(outs, axis=1)

    q_scale = HEAD_DIM ** -0.5
    acc_ref[...] = _dot(h, w_ref[:, :dq + dk])
    v_ref[...] = _dot(h, w_ref[:, dq + dk:]).astype(BF16)
    for c0 in range(0, dq + dk, cw):
        acc = acc_ref[:, c0:c0 + cw]
        if c0 < dq:
            q_ref[:, c0:c0 + cw] = norm_rope(acc, qg_ref[...], q_scale)
        else:
            k_ref[:, c0 - dq:c0 - dq + cw] = norm_rope(acc, kg_ref[...], 1.0)


def _qkv_proj(x, pos, gain, w, q_gain, k_gain, side_a, side_b, *, bm):
    t, d = x.shape
    dq = d
    dk = d // ATT_GROUP
    cw = _pick(dk, (256, 128))
    inv_freq = ROPE_THETA ** (-jnp.arange(0, HEAD_DIM, 2, dtype=F32) / HEAD_DIM)
    invf = jnp.tile(inv_freq, LANES // (HEAD_DIM // 2))[None, :]
    tile_gain = lambda g: jnp.tile(g.astype(F32), LANES // HEAD_DIM)[None, :]
    row = lambda i: (i, 0)
    fixed = lambda i: (0, 0)
    spec_a, shape_a = _ride_along_cast(side_a, t // bm, lambda i: i)
    spec_b, shape_b = _ride_along_cast(side_b, t // bm, lambda i: i)
    return pl.pallas_call(
        functools.partial(_qkv_kernel, dq=dq, dk=dk, cw=cw),
        out_shape=(jax.ShapeDtypeStruct((t, dq), BF16),
                   jax.ShapeDtypeStruct((t, dk), BF16),
                   jax.ShapeDtypeStruct((t, dk), BF16),
                   shape_a, shape_b),
        grid=(t // bm,),
        in_specs=[pl.BlockSpec((bm, d), row),
                  pl.BlockSpec((bm, 1), row),
                  pl.BlockSpec((1, d), fixed),
                  pl.BlockSpec((d, dq + 2 * dk), fixed),
                  pl.BlockSpec((1, LANES), fixed),
                  pl.BlockSpec((1, LANES), fixed),
                  pl.BlockSpec((1, LANES), fixed),
                  spec_a, spec_b],
        out_specs=(pl.BlockSpec((bm, dq), row),
                   pl.BlockSpec((bm, dk), row),
                   pl.BlockSpec((bm, dk), row),
                   spec_a, spec_b),
        scratch_shapes=[pltpu.VMEM((bm, dq + dk), F32)],
        compiler_params=_params("arbitrary"),
        name="qkv_proj",
    )(x, pos, gain[None, :], w, invf, tile_gain(q_gain), tile_gain(k_gain), side_a, side_b)


def _attn_kernel(sink_ref, q_ref, kp_ref, ko_ref, vp_ref, vo_ref, side_a_ref, side_b_ref, side_c_ref,
                 o_ref, side_a16_ref, side_b16_ref, side_c16_ref, s_ref, *, n_kv):
    side_a16_ref[...] = side_a_ref[...].astype(BF16)
    side_b16_ref[...] = side_b_ref[...].astype(BF16)
    side_c16_ref[...] = side_c_ref[...].astype(BF16)
    n = pl.program_id(1)
    W, pairs, slab = WINDOW, ATT_GROUP // 2, 2 * HEAD_DIM
    qi = lax.broadcasted_iota(jnp.int32, (W, 2 * W), 0)
    kj = lax.broadcasted_iota(jnp.int32, (W, 2 * W), 1)
    lag = qi + W - kj
    first_key = jnp.where(n > 0, 0, W)
    valid = ((lag >= 0) & (lag < W) & (kj >= first_key))[None]
    sink_col = kj[:1] == 0
    key_row = lax.broadcasted_iota(jnp.int32, (2 * W, 1), 0)
    zeros = jnp.zeros((2 * W, HEAD_DIM), BF16)
    ones = jnp.ones((2 * W, HEAD_DIM), BF16)
    v_wide = []
    for kv in range(n_kv):
        ks = slice(kv * HEAD_DIM, (kv + 1) * HEAD_DIM)
        k = jnp.concatenate([kp_ref[:, ks], ko_ref[:, ks]], axis=0)
        v = jnp.concatenate([vp_ref[:, ks], vo_ref[:, ks]], axis=0)
        v = jnp.where(key_row == 0, jnp.zeros_like(v), v)
        v_wide.append((jnp.concatenate([v, zeros, ones, zeros], axis=1),
                       jnp.concatenate([zeros, v, zeros, ones], axis=1)))
        c0 = kv * ATT_GROUP * HEAD_DIM
        qp = jnp.concatenate([q_ref[:, c0 + p * slab:c0 + (p + 1) * slab] for p in range(pairs)],
                             axis=0)
        s_ref[2 * kv] = _dot_nt(qp, jnp.concatenate([k, zeros], axis=1))
        s_ref[2 * kv + 1] = _dot_nt(qp, jnp.concatenate([zeros, k], axis=1))

    for kv in range(n_kv):
        c0 = kv * ATT_GROUP * HEAD_DIM

        def probs(parity):
            sink = jnp.concatenate(
                [jnp.full((1, 1, 1), sink_ref[kv * ATT_GROUP + 2 * p + parity], F32)
                 for p in range(pairs)], axis=0)
            fill = jnp.where(sink_col[None], sink, -jnp.inf)
            s = jnp.where(valid, s_ref[2 * kv + parity].reshape(pairs, W, 2 * W), fill)
            m = jnp.max(s, axis=-1, keepdims=True)
            return jnp.exp(s - m).astype(BF16).reshape(pairs * W, 2 * W)

        nd = _dot(probs(0), v_wide[kv][0]) + _dot(probs(1), v_wide[kv][1])
        out = (nd[:, :slab] / nd[:, slab:]).astype(BF16)
        for p in range(pairs):
            o_ref[:, c0 + p * slab:c0 + (p + 1) * slab] = out[p * W:(p + 1) * W]


def _attention(q, k, v, sinks, side_a, side_b, side_c, *, batch):
    t, dq = q.shape
    dk = k.shape[1]
    nb = t // batch // WINDOW
    own = lambda b, n: (b * nb + n, 0)
    prev = lambda b, n: (b * nb + jnp.maximum(n - 1, 0), 0)
    step = lambda b, n: b * nb + n
    spec_a, shape_a = _ride_along_cast(side_a, batch * nb, step)
    spec_b, shape_b = _ride_along_cast(side_b, batch * nb, step)
    spec_c, shape_c = _ride_along_cast(side_c, batch * nb, step)
    return pl.pallas_call(
        functools.partial(_attn_kernel, n_kv=dk // HEAD_DIM),
        out_shape=(jax.ShapeDtypeStruct((t, dq), BF16), shape_a, shape_b, shape_c),
        grid=(batch, nb),
        in_specs=[pl.BlockSpec(memory_space=pltpu.SMEM),
                  pl.BlockSpec((WINDOW, dq), own),
                  pl.BlockSpec((WINDOW, dk), prev),
                  pl.BlockSpec((WINDOW, dk), own),
                  pl.BlockSpec((WINDOW, dk), prev),
                  pl.BlockSpec((WINDOW, dk), own),
                  spec_a, spec_b, spec_c],
        out_specs=(pl.BlockSpec((WINDOW, dq), own), spec_a, spec_b, spec_c),
        scratch_shapes=[pltpu.VMEM((2 * (dk // HEAD_DIM), ATT_GROUP // 2 * WINDOW, 2 * WINDOW), F32)],
        compiler_params=_params("arbitrary", "arbitrary"),
        name="swa_attention",
    )(sinks.astype(F32), q, k, k, v, v, side_a, side_b, side_c)


def _proj_residual_kernel(x_ref, a_ref, w_ref, o_ref):
    o_ref[...] = x_ref[...] + _dot(a_ref[...], w_ref[...])


def _proj_residual(x, a, w, *, bm, bn):
    t, d = x.shape
    k = a.shape[1]
    return pl.pallas_call(
        _proj_residual_kernel,
        out_shape=jax.ShapeDtypeStruct((t, d), F32),
        grid=(d // bn, t // bm),
        in_specs=[pl.BlockSpec((bm, bn), lambda j, i: (i, j)),
                  pl.BlockSpec((bm, k), lambda j, i: (i, 0)),
                  pl.BlockSpec((k, bn), lambda j, i: (0, j))],
        out_specs=pl.BlockSpec((bm, bn), lambda j, i: (i, j)),
        compiler_params=_params("arbitrary", "arbitrary"),
        name="proj_residual",
    )(x, a, w)


def _ffn_kernel(x_ref, g_ref, wg_ref, wu_ref, wd_ref, o_ref, h_ref):
    f = pl.program_id(1)

    @pl.when(f == 0)
    def _():
        x = x_ref[...]
        h_ref[...] = _rms_norm_bf16(x, g_ref[...])
        o_ref[...] = x

    h = h_ref[...]
    gate = _dot(h, wg_ref[...])
    up = _dot(h, wu_ref[...])
    act = (gate * jax.nn.sigmoid(gate) * up).astype(BF16)
    o_ref[...] += _dot(act, wd_ref[...])


def _ffn(x, gain, w_gate, w_up, w_down, *, layer, bm, bf):
    t, d = x.shape
    dff = w_gate.shape[2]
    return pl.pallas_call(
        _ffn_kernel,
        out_shape=jax.ShapeDtypeStruct((t, d), F32),
        grid=(t // bm, dff // bf),
        in_specs=[pl.BlockSpec((bm, d), lambda i, f: (i, 0)),
                  pl.BlockSpec((1, d), lambda i, f: (0, 0)),
                  pl.BlockSpec((None, d, bf), lambda i, f: (layer, 0, f)),
                  pl.BlockSpec((None, d, bf), lambda i, f: (layer, 0, f)),
                  pl.BlockSpec((None, bf, d), lambda i, f: (layer, f, 0))],
        out_specs=pl.BlockSpec((bm, d), lambda i, f: (i, 0)),
        scratch_shapes=[pltpu.VMEM((bm, d), BF16)],
        compiler_params=_params("arbitrary", "arbitrary"),
        name="swiglu_ffn",
    )(x, gain[None, :], w_gate, w_up, w_down)


def _softplus(x):
    return jnp.maximum(x, 0.0) + jnp.log1p(jnp.exp(-jnp.abs(x)))


def _silu(x):
    return x * jax.nn.sigmoid(x)


def _bf16_terms(x, n):
    terms = []
    for _ in range(n):
        t = x.astype(BF16).astype(F32)
        terms.append(t)
        x = x - t
    return terms


def _ssm_gate_kernel(x_ref, g_ref, w_ref, wdtt_ref, dtb_ref, alog_col_ref,
                     zs_ref, h_ref, acum_t_ref, acum_c_ref, dt_c_ref, *, chunk):
    @pl.when(pl.program_id(1) == 0)
    def _():
        h = _rms_norm_bf16(x_ref[...], g_ref[...])
        h_ref[...] = h
        heads = wdtt_ref.shape[0]
        L = chunk
        dt_t = _softplus(_dot_nt(wdtt_ref[...], h) + dtb_ref[...])
        dta_t = dt_t * (-LOG2_E * jnp.exp(alog_col_ref[...]))
        r_i = lax.broadcasted_iota(jnp.int32, (L, L), 0)
        c_i = lax.broadcasted_iota(jnp.int32, (L, L), 1)
        upper = (r_i <= c_i).astype(BF16)
        for r0 in range(0, x_ref.shape[0], L):
            rows = slice(r0, r0 + L)
            stacked = jnp.concatenate(_bf16_terms(dta_t[:, rows], 3), axis=0).astype(BF16)
            sums = _dot(stacked, upper)
            acum_t = sums[:heads] + sums[heads:2 * heads] + sums[2 * heads:]
            acum_t_ref[:, rows] = acum_t
            acum_c_ref[rows, :] = acum_t.T
        dt_c_ref[...] = dt_t.T

    zs_ref[...] = _silu(_dot(h_ref[...], w_ref[...])).astype(BF16)


def _ssm_gate_proj(x, gain, w_in, w_dt, dt_bias, a_log, *, bm, bn, n):
    t, d = x.shape
    heads = w_dt.shape[1]
    fixed = lambda i, j: (0, 0)
    row_tile = lambda i, j: (i, 0)
    f32 = lambda v: v.astype(F32)
    return pl.pallas_call(
        functools.partial(_ssm_gate_kernel, chunk=SSM_CHUNK),
        out_shape=(jax.ShapeDtypeStruct((t, n), BF16),
                   jax.ShapeDtypeStruct((t, d), BF16),
                   jax.ShapeDtypeStruct((heads, t), F32),
                   jax.ShapeDtypeStruct((t, heads), F32),
                   jax.ShapeDtypeStruct((t, heads), F32)),
        grid=(t // bm, n // bn),
        in_specs=[pl.BlockSpec((bm, d), row_tile),
                  pl.BlockSpec((1, d), fixed),
                  pl.BlockSpec((d, bn), lambda i, j: (0, j)),
                  pl.BlockSpec((heads, d), fixed),
                  pl.BlockSpec((heads, 1), fixed),
                  pl.BlockSpec((heads, 1), fixed)],
        out_specs=(pl.BlockSpec((bm, bn), lambda i, j: (i, j)),
                   pl.BlockSpec((bm, d), row_tile),
                   pl.BlockSpec((heads, bm), lambda i, j: (0, i)),
                   pl.BlockSpec((bm, heads), row_tile),
                   pl.BlockSpec((bm, heads), row_tile)),
        compiler_params=_params("arbitrary", "arbitrary"),
        name="ssm_gate_proj",
    )(x, gain[None, :], w_in, w_dt.T, f32(dt_bias)[:, None], f32(a_log)[:, None])


def _ssm_conv_kernel(h_ref, w_ref, cw_ref, cb_ref, o_ref, acc_ref, carry_ref, *, tiles_per_seq):
    i = pl.program_id(0)
    j = pl.program_id(1)
    bm = h_ref.shape[0]
    acc_ref[...] = _dot(h_ref[...], w_ref[...])
    acc = acc_ref[...]
    row8 = lax.broadcasted_iota(jnp.int32, (8, 1), 0)
    tail = jnp.where(i % tiles_per_seq != 0, carry_ref[j], 0.0)
    carry_ref[j] = acc[bm - 8:, :]
    def shift_rows(v, v_tail, back):
        rolled = pltpu.roll(v, back, 0)
        head = jnp.where(row8 < back, pltpu.roll(v_tail, back, 0), rolled[:8])
        return jnp.concatenate([head, rolled[8:]], axis=0)

    w0, w1, w2, w3 = (cw_ref[k:k + 1, :] for k in range(SSM_CONV))
    a1 = shift_rows(acc, tail, 1)
    q = w1 * acc + w0 * a1
    q_tail = w1 * tail + w0 * pltpu.roll(tail, 1, 0)
    out = cb_ref[...] + w3 * acc + w2 * a1 + shift_rows(q, q_tail, 2)
    o_ref[...] = _silu(out).astype(BF16)


def _ssm_conv_proj(h, w_in, conv_w, conv_b, *, bm, bn, seq, col0, n):
    t, d = h.shape
    return pl.pallas_call(
        functools.partial(_ssm_conv_kernel, tiles_per_seq=seq // bm),
        out_shape=jax.ShapeDtypeStruct((t, n), BF16),
        grid=(t // bm, n // bn),
        in_specs=[pl.BlockSpec((bm, d), lambda i, j: (i, 0)),
                  pl.BlockSpec((d, bn), lambda i, j: (0, col0 + j)),
                  pl.BlockSpec((SSM_CONV, bn), lambda i, j: (0, j)),
                  pl.BlockSpec((1, bn), lambda i, j: (0, j))],
        out_specs=pl.BlockSpec((bm, bn), lambda i, j: (i, j)),
        scratch_shapes=[pltpu.VMEM((bm, bn), F32), pltpu.VMEM((n // bn, 8, bn), F32)],
        compiler_params=_params("arbitrary", "arbitrary"),
        name="ssm_conv_proj",
    )(h, w_in, conv_w.astype(F32), conv_b.astype(F32)[None, :])


def _ssd_kernel(zs_ref, xs_ref, b_ref, c_ref, acum_t_ref, acum_c_ref, dt_c_ref, ex_ref,
                d_ref, ng_ref, y_ref, state_ref, *, chunk, rpg, groups):
    c = pl.program_id(1)
    L, H, N = chunk, chunk // 2, SSM_STATE
    gw = rpg * HEAD_DIM
    tri = (lax.broadcasted_iota(jnp.int32, (H, H), 0) >= lax.broadcasted_iota(jnp.int32, (H, H), 1))

    @pl.when(c == 0)
    def _():
        state_ref[...] = jnp.zeros(state_ref.shape, F32)

    acum_c = acum_c_ref[...]
    dt_c = dt_c_ref[...]
    grow = jnp.exp2(acum_c)
    to_end = dt_c * jnp.exp2(acum_c[L - 1:L, :] - acum_c)
    stack = lambda v: jnp.concatenate(_bf16_terms(v, 2), axis=1).astype(BF16)
    dt_s, grow_s, to_end_s = stack(dt_c), stack(grow), stack(to_end)

    for u in range(groups):
        lanes = slice(u * gw, (u + 1) * gw)
        xs = xs_ref[:, lanes].astype(F32)
        bm = b_ref[:, u * N:(u + 1) * N]
        cm = c_ref[:, u * N:(u + 1) * N]
        ex = ex_ref[u]
        dt_full = _dot(dt_s, ex)
        grow_full = _dot(grow_s, ex)
        to_end_full = _dot(to_end_s, ex)
        xdt_b = (xs * dt_full).astype(BF16)
        cb = _dot_nt(cm, bm)
        y_parts = []
        for r in range(rpg):
            head = u * rpg + r
            col = acum_c[:, head:head + 1]
            row = acum_t_ref[head:head + 1, :]
            d00 = jnp.exp2(jnp.where(tri, col[:H] - row[:, :H], -jnp.inf))
            d11 = jnp.exp2(jnp.where(tri, col[H:] - row[:, H:], -jnp.inf))
            d10 = jnp.exp2(col[H:] - row[:, :H])
            m00 = (cb[:H, :H] * d00).astype(BF16)
            m1 = jnp.concatenate([cb[H:, :H] * d10, cb[H:, H:] * d11], axis=1).astype(BF16)
            x_r = xdt_b[:, r * HEAD_DIM:(r + 1) * HEAD_DIM]
            y_parts.append(jnp.concatenate([_dot(m00, x_r[:H]), _dot(m1, x_r)], axis=0))
        y = jnp.concatenate(y_parts, axis=1)

        state = state_ref[u]
        y = y + _dot(cm, state.astype(BF16)) * grow_full
        state_ref[u] = state * grow_full[L - 1:L, :] + _dot_tn(bm, (xs * to_end_full).astype(BF16))

        y = (y + d_ref[:, lanes] * xs) * zs_ref[:, lanes].astype(F32)
        y = y * lax.rsqrt(jnp.mean(y * y, axis=-1, keepdims=True) + EPS)
        y_ref[:, lanes] = (y * ng_ref[:, lanes]).astype(BF16)


def _ssd(zs, xbc, acum_t, acum_c, dt_c, d_skip, norm_g, *, batch):
    t, d_inner = zs.shape
    heads = acum_t.shape[0]
    G, N = SSM_GROUPS, SSM_STATE
    rpg = heads // G
    gw = d_inner // G
    L = SSM_CHUNK
    nc = t // batch // L
    rows = lambda b, c: (b * nc + c, 0)
    fixed = lambda b, c: (0, 0)
    lane_head = jnp.arange(G)[:, None, None] * rpg + jnp.arange(gw)[None, None, :] // HEAD_DIM
    one_hot = (lane_head == jnp.arange(heads)[None, :, None]).astype(BF16)
    rep = lambda v: jnp.repeat(v.astype(F32), HEAD_DIM)[None, :]
    bc0 = d_inner // (G * N)
    return pl.pallas_call(
        functools.partial(_ssd_kernel, chunk=L, rpg=rpg, groups=G),
        out_shape=jax.ShapeDtypeStruct((t, d_inner), BF16),
        grid=(batch, nc),
        in_specs=[pl.BlockSpec((L, d_inner), rows),
                  pl.BlockSpec((L, d_inner), rows),
                  pl.BlockSpec((L, G * N), lambda b, c: (b * nc + c, bc0)),
                  pl.BlockSpec((L, G * N), lambda b, c: (b * nc + c, bc0 + 1)),
                  pl.BlockSpec((heads, L), lambda b, c: (0, b * nc + c)),
                  pl.BlockSpec((L, heads), rows),
                  pl.BlockSpec((L, heads), rows),
                  pl.BlockSpec((G, 2 * heads, gw), lambda b, c: (0, 0, 0)),
                  pl.BlockSpec((1, d_inner), fixed),
                  pl.BlockSpec((1, d_inner), fixed)],
        out_specs=pl.BlockSpec((L, d_inner), rows),
        scratch_shapes=[pltpu.VMEM((G, N, gw), F32)],
        compiler_params=_params("arbitrary", "arbitrary"),
        name="ssd_scan",
    )(zs, xbc, xbc, xbc, acum_t, acum_c, dt_c, jnp.tile(one_hot, (1, 2, 1)),
      rep(d_skip), norm_g.astype(F32)[None, :])


def kernel(x, positions, mixer_norm, ffn_norm, attn_w_qkv, attn_q_norm, attn_k_norm, attn_sinks,
           attn_w_o, ssm_w_in, ssm_conv_w, ssm_conv_b, ssm_dt_bias, ssm_a_log, ssm_d, ssm_norm,
           ssm_w_out, ffn_w_gate, ffn_w_up, ffn_w_down):
    batch, seq, d = x.shape
    t = batch * seq
    xf = x.reshape(t, d)
    pos = positions.reshape(t, 1)
    d_inner = ssm_w_out.shape[1]
    main_w = 2 * d_inner + 2 * SSM_GROUPS * SSM_STATE
    bm = _pick(t, (512, 256, 128))
    bm_big = _pick(seq, (1024, 512, 256, 128))
    bf = _pick(ffn_w_gate.shape[2], (512, 256))
    bn = _pick(d, (1024, 512))

    def ffn(xf, i):
        return _ffn(xf, ffn_norm[i], w_gate, w_up, w_down, layer=i, bm=bm_big, bf=bf)

    q, k, v, w_down, w_o = _qkv_proj(xf, pos, mixer_norm[0], attn_w_qkv[0].astype(BF16),
                                     attn_q_norm[0], attn_k_norm[0], ffn_w_down, attn_w_o, bm=bm)
    att, w_gate, w_up, w_out = _attention(q, k, v, attn_sinks[0], ffn_w_gate, ffn_w_up, ssm_w_out,
                                          batch=batch)
    xf = _proj_residual(xf, att, w_o[0], bm=bm_big, bn=bn)
    xf = ffn(xf, 0)

    w_in = ssm_w_in[0].astype(BF16)
    bn_in = _pick(SSM_GROUPS * SSM_STATE, (1024, 512, 256))
    zs, h, acum_t, acum_c, dt_c = _ssm_gate_proj(
        xf, mixer_norm[1], w_in, w_in[:, main_w:], ssm_dt_bias[0], ssm_a_log[0],
        bm=bm_big, bn=bn_in, n=d_inner)
    xbc = _ssm_conv_proj(h, w_in, ssm_conv_w[0], ssm_conv_b[0],
                         bm=bm_big, bn=bn_in, seq=seq, col0=d_inner // bn_in, n=main_w - d_inner)
    y = _ssd(zs, xbc, acum_t, acum_c, dt_c, ssm_d[0], ssm_norm[0], batch=batch)
    xf = _proj_residual(xf, y, w_out[0], bm=bm, bn=bn)
    xf = ffn(xf, 1)
    return xf.reshape(batch, seq, d)
```

```python
import functools

import jax
import jax.numpy as jnp
from jax import lax
from jax.experimental import pallas as pl
from jax.experimental.pallas import tpu as pltpu

EPS = 1e-6
HEAD_DIM = 64
ATT_GROUP = 8
WINDOW = 128
ROPE_THETA = 10000.0
SSM_GROUPS = 8
SSM_STATE = 128
SSM_CONV = 4
SSM_CHUNK = 256
LOG2_E = 1.4426950408889634
LANES = 128
VMEM_LIMIT_BYTES = 56 * 1024 * 1024

F32 = jnp.float32
BF16 = jnp.bfloat16


def _params(*sem):
    return pltpu.CompilerParams(dimension_semantics=sem, vmem_limit_bytes=VMEM_LIMIT_BYTES)


def _pick(n, candidates):
    for c in candidates:
        if n % c == 0:
            return c
    raise ValueError(f"no tile in {candidates} divides {n}")


def _ride_along_cast(w, steps, step_index):
    layers, rows, cols = w.shape
    slab = layers * rows // steps
    per_layer = rows // slab
    assert slab * steps == layers * rows and per_layer * slab == rows and slab % 16 == 0
    imap = lambda *g: (step_index(*g) // per_layer, step_index(*g) % per_layer, 0)
    spec = pl.BlockSpec((None, slab, cols), imap)
    return spec, jax.ShapeDtypeStruct(w.shape, BF16)


def _rms_norm_bf16(x, gain):
    y = x * lax.rsqrt(jnp.mean(x * x, axis=-1, keepdims=True) + EPS)
    return (y * gain).astype(BF16)


def _dot(a, b):
    return jnp.dot(a, b, preferred_element_type=F32)


def _dot_nt(a, b):
    return lax.dot_general(a, b, (((1,), (1,)), ((), ())), preferred_element_type=F32)


def _dot_tn(a, b):
    return lax.dot_general(a, b, (((0,), (0,)), ((), ())), preferred_element_type=F32)


def _qkv_kernel(x_ref, pos_ref, g_ref, w_ref, invf_ref, qg_ref, kg_ref, side_a_ref, side_b_ref,
                q_ref, k_ref, v_ref, side_a16_ref, side_b16_ref, acc_ref, *, dq, dk, cw):
    side_a16_ref[...] = side_a_ref[...].astype(BF16)
    side_b16_ref[...] = side_b_ref[...].astype(BF16)
    h = _rms_norm_bf16(x_ref[...], g_ref[...])
    ang = pos_ref[...].astype(F32) * invf_ref[...]
    lane = lax.broadcasted_iota(jnp.int32, (1, LANES), 1)
    first_half = (lane % HEAD_DIM) < (HEAD_DIM // 2)
    cos = jnp.cos(ang)
    sin = jnp.sin(ang)
    sin_signed = jnp.where(first_half, -sin, sin)

    r_i = lax.broadcasted_iota(jnp.int32, (cw, cw), 0)
    c_i = lax.broadcasted_iota(jnp.int32, (cw, cw), 1)
    same_head = (r_i // HEAD_DIM == c_i // HEAD_DIM).astype(BF16)

    def norm_rope(a, gain, scale):
        sq = a * a
        sq_hi = sq.astype(BF16)
        sq_lo = (sq - sq_hi.astype(F32)).astype(BF16)
        ss = _dot(sq_hi, same_head) + _dot(sq_lo, same_head)
        outs = []
        for s0 in range(0, cw, LANES):
            sl = slice(s0, s0 + LANES)
            y = a[:, sl] * lax.rsqrt(ss[:, sl] * (1.0 / HEAD_DIM) + EPS) * gain
            partner = jnp.where(first_half,
                                pltpu.roll(y, LANES - HEAD_DIM // 2, 1),
                                pltpu.roll(y, HEAD_DIM // 2, 1))
            outs.append(((y * cos + partner * sin_signed) * scale).astype(BF16))
        return jnp.concatenate(outs, axis=1)

    q_scale = HEAD_DIM ** -0.5
    acc_ref[...] = _dot(h, w_ref[:, :dq + dk])
    v_ref[...] = _dot(h, w_ref[:, dq + dk:]).astype(BF16)
    for c0 in range(0, dq + dk, cw):
        acc = acc_ref[:, c0:c0 + cw]
        if c0 < dq:
            q_ref[:, c0:c0 + cw] = norm_rope(acc, qg_ref[...], q_scale)
        else:
            k_ref[:, c0 - dq:c0 - dq + cw] = norm_rope(acc, kg_ref[...], 1.0)


def _qkv_proj(x, pos, gain, w, q_gain, k_gain, side_a, side_b, *, bm):
    t, d = x.shape
    dq = d
    dk = d // ATT_GROUP
    cw = _pick(dk, (256, 128))
    inv_freq = ROPE_THETA ** (-jnp.arange(0, HEAD_DIM, 2, dtype=F32) / HEAD_DIM)
    invf = jnp.tile(inv_freq, LANES // (HEAD_DIM // 2))[None, :]
    tile_gain = lambda g: jnp.tile(g.astype(F32), LANES // HEAD_DIM)[None, :]
    row = lambda i: (i, 0)
    fixed = lambda i: (0, 0)
    spec_a, shape_a = _ride_along_cast(side_a, t // bm, lambda i: i)
    spec_b, shape_b = _ride_along_cast(side_b, t // bm, lambda i: i)
    return pl.pallas_call(
        functools.partial(_qkv_kernel, dq=dq, dk=dk, cw=cw),
        out_shape=(jax.ShapeDtypeStruct((t, dq), BF16),
                   jax.ShapeDtypeStruct((t, dk), BF16),
                   jax.ShapeDtypeStruct((t, dk), BF16),
                   shape_a, shape_b),
        grid=(t // bm,),
        in_specs=[pl.BlockSpec((bm, d), row),
                  pl.BlockSpec((bm, 1), row),
                  pl.BlockSpec((1, d), fixed),
                  pl.BlockSpec((d, dq + 2 * dk), fixed),
                  pl.BlockSpec((1, LANES), fixed),
                  pl.BlockSpec((1, LANES), fixed),
                  pl.BlockSpec((1, LANES), fixed),
                  spec_a, spec_b],
        out_specs=(pl.BlockSpec((bm, dq), row),
                   pl.BlockSpec((bm, dk), row),
                   pl.BlockSpec((bm, dk), row),
                   spec_a, spec_b),
        scratch_shapes=[pltpu.VMEM((bm, dq + dk), F32)],
        compiler_params=_params("arbitrary"),
        name="qkv_proj",
    )(x, pos, gain[None, :], w, invf, tile_gain(q_gain), tile_gain(k_gain), side_a, side_b)


def _attn_kernel(sink_ref, q_ref, kp_ref, kc_ref, vp_ref, vc_ref, side_a_ref, side_b_ref, side_c_ref,
                 o_ref, side_a16_ref, side_b16_ref, side_c16_ref, s_ref, *, n_kv, blocks):
    side_a16_ref[...] = side_a_ref[...].astype(BF16)
    side_b16_ref[...] = side_b_ref[...].astype(BF16)
    side_c16_ref[...] = side_c_ref[...].astype(BF16)
    n = pl.program_id(1)
    W, pairs, slab = WINDOW, ATT_GROUP // 2, 2 * HEAD_DIM
    qi = lax.broadcasted_iota(jnp.int32, (W, 2 * W), 0)
    kj = lax.broadcasted_iota(jnp.int32, (W, 2 * W), 1)
    lag = qi + W - kj
    band = (lag >= 0) & (lag < W)
    sink_col = kj[:1] == 0
    key_row = lax.broadcasted_iota(jnp.int32, (2 * W, 1), 0)
    zeros = jnp.zeros((2 * W, HEAD_DIM), BF16)
    ones = jnp.ones((2 * W, HEAD_DIM), BF16)

    def rows(j):
        return slice(j * W, (j + 1) * W)

    v_wide = {}
    for j in range(blocks):
        for kv in range(n_kv):
            ks = slice(kv * HEAD_DIM, (kv + 1) * HEAD_DIM)
            k_prev = kp_ref[:, ks] if j == 0 else kc_ref[rows(j - 1), ks]
            v_prev = vp_ref[:, ks] if j == 0 else vc_ref[rows(j - 1), ks]
            k = jnp.concatenate([k_prev, kc_ref[rows(j), ks]], axis=0)
            v = jnp.concatenate([v_prev, vc_ref[rows(j), ks]], axis=0)
            v = jnp.where(key_row == 0, jnp.zeros_like(v), v)
            v_wide[j, kv] = (jnp.concatenate([v, zeros, ones, zeros], axis=1),
                             jnp.concatenate([zeros, v, zeros, ones], axis=1))
            c0 = kv * ATT_GROUP * HEAD_DIM
            qp = jnp.concatenate([q_ref[rows(j), c0 + p * slab:c0 + (p + 1) * slab]
                                  for p in range(pairs)], axis=0)
            slot = 2 * (j * n_kv + kv)
            s_ref[slot] = _dot_nt(qp, jnp.concatenate([k, zeros], axis=1))
            s_ref[slot + 1] = _dot_nt(qp, jnp.concatenate([zeros, k], axis=1))

    for j in range(blocks):
        first_key = jnp.where(n * blocks + j > 0, 0, W)
        valid = (band & (kj >= first_key))[None]
        for kv in range(n_kv):
            c0 = kv * ATT_GROUP * HEAD_DIM
            slot = 2 * (j * n_kv + kv)

            def probs(parity):
                sink = jnp.concatenate(
                    [jnp.full((1, 1, 1), sink_ref[kv * ATT_GROUP + 2 * p + parity], F32)
                     for p in range(pairs)], axis=0)
                fill = jnp.where(sink_col[None], sink, -jnp.inf)
                s = jnp.where(valid, s_ref[slot + parity].reshape(pairs, W, 2 * W), fill)
                m = jnp.max(s, axis=-1, keepdims=True)
                return jnp.exp(s - m).astype(BF16).reshape(pairs * W, 2 * W)

            nd = _dot(probs(0), v_wide[j, kv][0]) + _dot(probs(1), v_wide[j, kv][1])
            out = (nd[:, :slab] / nd[:, slab:]).astype(BF16)
            for p in range(pairs):
                o_ref[rows(j), c0 + p * slab:c0 + (p + 1) * slab] = out[p * W:(p + 1) * W]


def _attention(q, k, v, sinks, side_a, side_b, side_c, *, batch, blocks):
    t, dq = q.shape
    dk = k.shape[1]
    nb = t // batch // WINDOW // blocks
    own = lambda b, n: (b * nb + n, 0)
    prev = lambda b, n: ((b * nb + n) * blocks - jnp.minimum(n, 1), 0)
    step = lambda b, n: b * nb + n
    spec_a, shape_a = _ride_along_cast(side_a, batch * nb, step)
    spec_b, shape_b = _ride_along_cast(side_b, batch * nb, step)
    spec_c, shape_c = _ride_along_cast(side_c, batch * nb, step)
    rows = blocks * WINDOW
    return pl.pallas_call(
        functools.partial(_attn_kernel, n_kv=dk // HEAD_DIM, blocks=blocks),
        out_shape=(jax.ShapeDtypeStruct((t, dq), BF16), shape_a, shape_b, shape_c),
        grid=(batch, nb),
        in_specs=[pl.BlockSpec(memory_space=pltpu.SMEM),
                  pl.BlockSpec((rows, dq), own),
                  pl.BlockSpec((WINDOW, dk), prev),
                  pl.BlockSpec((rows, dk), own),
                  pl.BlockSpec((WINDOW, dk), prev),
                  pl.BlockSpec((rows, dk), own),
                  spec_a, spec_b, spec_c],
        out_specs=(pl.BlockSpec((rows, dq), own), spec_a, spec_b, spec_c),
        scratch_shapes=[pltpu.VMEM((2 * blocks * (dk // HEAD_DIM), ATT_GROUP // 2 * WINDOW, 2 * WINDOW), F32)],
        compiler_params=_params("arbitrary", "arbitrary"),
        name="swa_attention",
    )(sinks.astype(F32), q, k, k, v, v, side_a, side_b, side_c)


def _proj_residual_kernel(x_ref, a_ref, w_ref, o_ref):
    o_ref[...] = x_ref[...] + _dot(a_ref[...], w_ref[...])


def _proj_residual(x, a, w, *, bm, bn):
    t, d = x.shape
    k = a.shape[1]
    return pl.pallas_call(
        _proj_residual_kernel,
        out_shape=jax.ShapeDtypeStruct((t, d), F32),
        grid=(d // bn, t // bm),
        in_specs=[pl.BlockSpec((bm, bn), lambda j, i: (i, j)),
                  pl.BlockSpec((bm, k), lambda j, i: (i, 0)),
                  pl.BlockSpec((k, bn), lambda j, i: (0, j))],
        out_specs=pl.BlockSpec((bm, bn), lambda j, i: (i, j)),
        compiler_params=_params("arbitrary", "arbitrary"),
        name="proj_residual",
    )(x, a, w)


def _ffn_kernel(x_ref, g_ref, wg_ref, wu_ref, wd_ref, o_ref, h_ref):
    f = pl.program_id(1)

    @pl.when(f == 0)
    def _():
        x = x_ref[...]
        h_ref[...] = _rms_norm_bf16(x, g_ref[...])
        o_ref[...] = x

    h = h_ref[...]
    gate = _dot(h, wg_ref[...])
    up = _dot(h, wu_ref[...])
    act = (gate * jax.nn.sigmoid(gate) * up).astype(BF16)
    o_ref[...] += _dot(act, wd_ref[...])


def _ffn(x, gain, w_gate, w_up, w_down, *, layer, bm, bf):
    t, d = x.shape
    dff = w_gate.shape[2]
    return pl.pallas_call(
        _ffn_kernel,
        out_shape=jax.ShapeDtypeStruct((t, d), F32),
        grid=(t // bm, dff // bf),
        in_specs=[pl.BlockSpec((bm, d), lambda i, f: (i, 0)),
                  pl.BlockSpec((1, d), lambda i, f: (0, 0)),
                  pl.BlockSpec((None, d, bf), lambda i, f: (layer, 0, f)),
                  pl.BlockSpec((None, d, bf), lambda i, f: (layer, 0, f)),
                  pl.BlockSpec((None, bf, d), lambda i, f: (layer, f, 0))],
        out_specs=pl.BlockSpec((bm, d), lambda i, f: (i, 0)),
        scratch_shapes=[pltpu.VMEM((bm, d), BF16)],
        compiler_params=_params("arbitrary", "arbitrary"),
        name="swiglu_ffn",
    )(x, gain[None, :], w_gate, w_up, w_down)


def _softplus(x):
    return jnp.maximum(x, 0.0) + jnp.log1p(jnp.exp(-jnp.abs(x)))


def _silu(x):
    return x * jax.nn.sigmoid(x)


def _bf16_terms(x, n):
    terms = []
    for _ in range(n):
        t = x.astype(BF16).astype(F32)
        terms.append(t)
        x = x - t
    return terms


def _ssm_gate_kernel(x_ref, g_ref, w_ref, wdtt_ref, dtb_ref, alog_row_ref, alog_col_ref,
                     zs_ref, h_ref, acum_t_ref, acum_c_ref, dt_c_ref, *, chunk):
    @pl.when(pl.program_id(1) == 0)
    def _():
        h = _rms_norm_bf16(x_ref[...], g_ref[...])
        h_ref[...] = h
        heads = wdtt_ref.shape[0]
        L = chunk
        dt_t = _softplus(_dot_nt(wdtt_ref[...], h) + dtb_ref[...])
        dt_c = dt_t.T
        dta_c = dt_c * (-LOG2_E * jnp.exp(alog_row_ref[...]))
        dta_t = dt_t * (-LOG2_E * jnp.exp(alog_col_ref[...]))
        r_i = lax.broadcasted_iota(jnp.int32, (L, L), 0)
        c_i = lax.broadcasted_iota(jnp.int32, (L, L), 1)
        upper = (r_i <= c_i).astype(BF16)
        lower = (r_i >= c_i).astype(BF16)
        for r0 in range(0, x_ref.shape[0], L):
            rows = slice(r0, r0 + L)
            stacked = jnp.concatenate(_bf16_terms(dta_t[:, rows], 3), axis=0).astype(BF16)
            sums = _dot(stacked, upper)
            acum_t_ref[:, rows] = sums[:heads] + sums[heads:2 * heads] + sums[2 * heads:]
            acum_c = sum(_dot(lower, term.astype(BF16)) for term in _bf16_terms(dta_c[rows], 3))
            acum_c_ref[rows, :] = acum_c
        dt_c_ref[...] = dt_c

    zs_ref[...] = _silu(_dot(h_ref[...], w_ref[...])).astype(BF16)


def _ssm_gate_proj(x, gain, w_in, w_dt, dt_bias, a_log, *, bm, bn, n):
    t, d = x.shape
    heads = w_dt.shape[1]
    fixed = lambda i, j: (0, 0)
    row_tile = lambda i, j: (i, 0)
    f32 = lambda v: v.astype(F32)
    return pl.pallas_call(
        functools.partial(_ssm_gate_kernel, chunk=SSM_CHUNK),
        out_shape=(jax.ShapeDtypeStruct((t, n), BF16),
                   jax.ShapeDtypeStruct((t, d), BF16),
                   jax.ShapeDtypeStruct((heads, t), F32),
                   jax.ShapeDtypeStruct((t, heads), F32),
                   jax.ShapeDtypeStruct((t, heads), F32)),
        grid=(t // bm, n // bn),
        in_specs=[pl.BlockSpec((bm, d), row_tile),
                  pl.BlockSpec((1, d), fixed),
                  pl.BlockSpec((d, bn), lambda i, j: (0, j)),
                  pl.BlockSpec((heads, d), fixed),
                  pl.BlockSpec((heads, 1), fixed),
                  pl.BlockSpec((1, heads), fixed),
                  pl.BlockSpec((heads, 1), fixed)],
        out_specs=(pl.BlockSpec((bm, bn), lambda i, j: (i, j)),
                   pl.BlockSpec((bm, d), row_tile),
                   pl.BlockSpec((heads, bm), lambda i, j: (0, i)),
                   pl.BlockSpec((bm, heads), row_tile),
                   pl.BlockSpec((bm, heads), row_tile)),
        compiler_params=_params("arbitrary", "arbitrary"),
        name="ssm_gate_proj",
    )(x, gain[None, :], w_in, w_dt.T, f32(dt_bias)[:, None], f32(a_log)[None, :], f32(a_log)[:, None])


def _ssm_conv_kernel(h_ref, w_ref, cw_ref, cb_ref, o_ref, acc_ref, carry_ref, *, tiles_per_seq):
    i = pl.program_id(0)
    j = pl.program_id(1)
    bm = h_ref.shape[0]
    acc_ref[...] = _dot(h_ref[...], w_ref[...])
    acc = acc_ref[...]
    row8 = lax.broadcasted_iota(jnp.int32, (8, 1), 0)
    tail = jnp.where(i % tiles_per_seq != 0, carry_ref[j], 0.0)
    carry_ref[j] = acc[bm - 8:, :]
    def shift_rows(v, v_tail, back):
        rolled = pltpu.roll(v, back, 0)
        head = jnp.where(row8 < back, pltpu.roll(v_tail, back, 0), rolled[:8])
        return jnp.concatenate([head, rolled[8:]], axis=0)

    w0, w1, w2, w3 = (cw_ref[k:k + 1, :] for k in range(SSM_CONV))
    a1 = shift_rows(acc, tail, 1)
    q = w1 * acc + w0 * a1
    q_tail = w1 * tail + w0 * pltpu.roll(tail, 1, 0)
    out = cb_ref[...] + w3 * acc + w2 * a1 + shift_rows(q, q_tail, 2)
    o_ref[...] = _silu(out).astype(BF16)


def _ssm_conv_proj(h, w_in, conv_w, conv_b, *, bm, bn, seq, col0, n):
    t, d = h.shape
    return pl.pallas_call(
        functools.partial(_ssm_conv_kernel, tiles_per_seq=seq // bm),
        out_shape=jax.ShapeDtypeStruct((t, n), BF16),
        grid=(t // bm, n // bn),
        in_specs=[pl.BlockSpec((bm, d), lambda i, j: (i, 0)),
                  pl.BlockSpec((d, bn), lambda i, j: (0, col0 + j)),
                  pl.BlockSpec((SSM_CONV, bn), lambda i, j: (0, j)),
                  pl.BlockSpec((1, bn), lambda i, j: (0, j))],
        out_specs=pl.BlockSpec((bm, bn), lambda i, j: (i, j)),
        scratch_shapes=[pltpu.VMEM((bm, bn), F32), pltpu.VMEM((n // bn, 8, bn), F32)],
        compiler_params=_params("arbitrary", "arbitrary"),
        name="ssm_conv_proj",
    )(h, w_in, conv_w.astype(F32), conv_b.astype(F32)[None, :])


def _ssd_kernel(zs_ref, xs_ref, b_ref, c_ref, acum_t_ref, acum_c_ref, dt_c_ref, ex_ref,
                d_ref, ng_ref, y_ref, state_ref, *, chunk, rpg, groups):
    c = pl.program_id(1)
    L, H, N = chunk, chunk // 2, SSM_STATE
    gw = rpg * HEAD_DIM
    tri = (lax.broadcasted_iota(jnp.int32, (H, H), 0) >= lax.broadcasted_iota(jnp.int32, (H, H), 1))

    @pl.when(c == 0)
    def _():
        state_ref[...] = jnp.zeros(state_ref.shape, F32)

    acum_c = acum_c_ref[...]
    dt_c = dt_c_ref[...]
    grow = jnp.exp2(acum_c)
    to_end = dt_c * jnp.exp2(acum_c[L - 1:L, :] - acum_c)
    stack = lambda v: jnp.concatenate(_bf16_terms(v, 2), axis=1).astype(BF16)
    dt_s, grow_s, to_end_s = stack(dt_c), stack(grow), stack(to_end)

    for u in range(groups):
        lanes = slice(u * gw, (u + 1) * gw)
        xs = xs_ref[:, lanes].astype(F32)
        bm = b_ref[:, u * N:(u + 1) * N]
        cm = c_ref[:, u * N:(u + 1) * N]
        ex = ex_ref[u]
        dt_full = _dot(dt_s, ex)
        grow_full = _dot(grow_s, ex)
        to_end_full = _dot(to_end_s, ex)
        xdt_b = (xs * dt_full).astype(BF16)
        cb = _dot_nt(cm, bm)
        y_parts = []
        for r in range(rpg):
            head = u * rpg + r
            col = acum_c[:, head:head + 1]
            row = acum_t_ref[head:head + 1, :]
            d00 = jnp.exp2(jnp.where(tri, col[:H] - row[:, :H], -jnp.inf))
            d11 = jnp.exp2(jnp.where(tri, col[H:] - row[:, H:], -jnp.inf))
            d10 = jnp.exp2(col[H:] - row[:, :H])
            m00 = (cb[:H, :H] * d00).astype(BF16)
            m1 = jnp.concatenate([cb[H:, :H] * d10, cb[H:, H:] * d11], axis=1).astype(BF16)
            x_r = xdt_b[:, r * HEAD_DIM:(r + 1) * HEAD_DIM]
            y_parts.append(jnp.concatenate([_dot(m00, x_r[:H]), _dot(m1, x_r)], axis=0))
        y = jnp.concatenate(y_parts, axis=1)

        state = state_ref[u]
        y = y + _dot(cm, state.astype(BF16)) * grow_full
        state_ref[u] = state * grow_full[L - 1:L, :] + _dot_tn(bm, (xs * to_end_full).astype(BF16))

        y = (y + d_ref[:, lanes] * xs) * zs_ref[:, lanes].astype(F32)
        y = y * lax.rsqrt(jnp.mean(y * y, axis=-1, keepdims=True) + EPS)
        y_ref[:, lanes] = (y * ng_ref[:, lanes]).astype(BF16)


def _ssd(zs, xbc, acum_t, acum_c, dt_c, d_skip, norm_g, *, batch):
    t, d_inner = zs.shape
    heads = acum_t.shape[0]
    G, N = SSM_GROUPS, SSM_STATE
    rpg = heads // G
    gw = d_inner // G
    L = SSM_CHUNK
    nc = t // batch // L
    rows = lambda b, c: (b * nc + c, 0)
    fixed = lambda b, c: (0, 0)
    lane_head = jnp.arange(G)[:, None, None] * rpg + jnp.arange(gw)[None, None, :] // HEAD_DIM
    one_hot = (lane_head == jnp.arange(heads)[None, :, None]).astype(BF16)
    rep = lambda v: jnp.repeat(v.astype(F32), HEAD_DIM)[None, :]
    bc0 = d_inner // (G * N)
    return pl.pallas_call(
        functools.partial(_ssd_kernel, chunk=L, rpg=rpg, groups=G),
        out_shape=jax.ShapeDtypeStruct((t, d_inner), BF16),
        grid=(batch, nc),
        in_specs=[pl.BlockSpec((L, d_inner), rows),
                  pl.BlockSpec((L, d_inner), rows),
                  pl.BlockSpec((L, G * N), lambda b, c: (b * nc + c, bc0)),
                  pl.BlockSpec((L, G * N), lambda b, c: (b * nc + c, bc0 + 1)),
                  pl.BlockSpec((heads, L), lambda b, c: (0, b * nc + c)),
                  pl.BlockSpec((L, heads), rows),
                  pl.BlockSpec((L, heads), rows),
                  pl.BlockSpec((G, 2 * heads, gw), lambda b, c: (0, 0, 0)),
                  pl.BlockSpec((1, d_inner), fixed),
                  pl.BlockSpec((1, d_inner), fixed)],
        out_specs=pl.BlockSpec((L, d_inner), rows),
        scratch_shapes=[pltpu.VMEM((G, N, gw), F32)],
        compiler_params=_params("arbitrary", "arbitrary"),
        name="ssd_scan",
    )(zs, xbc, xbc, xbc, acum_t, acum_c, dt_c, jnp.tile(one_hot, (1, 2, 1)),
      rep(d_skip), norm_g.astype(F32)[None, :])


def kernel(x, positions, mixer_norm, ffn_norm, attn_w_qkv, attn_q_norm, attn_k_norm, attn_sinks,
           attn_w_o, ssm_w_in, ssm_conv_w, ssm_conv_b, ssm_dt_bias, ssm_a_log, ssm_d, ssm_norm,
           ssm_w_out, ffn_w_gate, ffn_w_up, ffn_w_down):
    batch, seq, d = x.shape
    t = batch * seq
    xf = x.reshape(t, d)
    pos = positions.reshape(t, 1)
    d_inner = ssm_w_out.shape[1]
    main_w = 2 * d_inner + 2 * SSM_GROUPS * SSM_STATE
    bm = _pick(t, (512, 256, 128))
    bm_big = _pick(seq, (1024, 512, 256, 128))
    bf = _pick(ffn_w_gate.shape[2], (512, 256))
    bn = _pick(d, (1024, 512))

    def ffn(xf, i):
        return _ffn(xf, ffn_norm[i], w_gate, w_up, w_down, layer=i, bm=bm_big, bf=bf)

    q, k, v, w_down, w_o = _qkv_proj(xf, pos, mixer_norm[0], attn_w_qkv[0].astype(BF16),
                                     attn_q_norm[0], attn_k_norm[0], ffn_w_down, attn_w_o, bm=bm)
    att, w_gate, w_up, w_out = _attention(q, k, v, attn_sinks[0], ffn_w_gate, ffn_w_up, ssm_w_out,
                                          batch=batch, blocks=2)
    xf = _proj_residual(xf, att, w_o[0], bm=bm_big, bn=bn)
    xf = ffn(xf, 0)

    w_in = ssm_w_in[0].astype(BF16)
    bn_in = _pick(SSM_GROUPS * SSM_STATE, (1024, 512, 256))
    zs, h, acum_t, acum_c, dt_c = _ssm_gate_proj(
        xf, mixer_norm[1], w_in, w_in[:, main_w:], ssm_dt_bias[0], ssm_a_log[0],
        bm=bm_big, bn=bn_in, n=d_inner)
    xbc = _ssm_conv_proj(h, w_in, ssm_conv_w[0], ssm_conv_b[0],
                         bm=bm_big, bn=bn_in, seq=seq, col0=d_inner // bn_in, n=main_w - d_inner)
    y = _ssd(zs, xbc, acum_t, acum_c, dt_c, ssm_d[0], ssm_norm[0], batch=batch)
    xf = _proj_residual(xf, y, w_out[0], bm=bm, bn=bn)
    xf = ffn(xf, 1)
    return xf.reshape(batch, seq, d)
```
